```python
import math
import jax
import jax.numpy as jnp
from jax import lax
import numpy as np

D_MODEL = 1024
BATCH = 8
SEQ = 2048
DEPTH = 2

F32 = jnp.float32

DN_ALPHA = (2.0 * DEPTH) ** 0.25
DN_BETA = (8.0 * DEPTH) ** -0.25
LN_EPS = 1e-5

A_WIDTH = D_MODEL // 2
A_HEADS = 4
A_VDIM = A_WIDTH // A_HEADS
A_QKDIM = A_VDIM // 2
A_COLS = 3 * A_WIDTH
A_QBLOCK = 128
A_NORM_EPS = 1e-5

B_WIDTH = D_MODEL // 2
B_HEAD = 64
B_HEADS = B_WIDTH // B_HEAD
B_LORA_W = 64
B_LORA_A = 64
B_LORA_G = 128
B_COLS = 3 * B_WIDTH + B_LORA_W + B_LORA_A + B_LORA_G
B_SPLITS = (B_WIDTH, 2 * B_WIDTH, 3 * B_WIDTH, 3 * B_WIDTH + B_LORA_W, 3 * B_WIDTH + B_LORA_W + B_LORA_A)
B_LNX_EPS = 64e-5

C_DK = 128
C_DV = 128
C_KHEADS = D_MODEL // 128
C_VHEADS = 2 * C_KHEADS
C_QK_W = C_KHEADS * C_DK
C_V_W = C_VHEADS * C_DV
C_QKV_W = 2 * C_QK_W + C_V_W
C_COLS = C_QKV_W + C_V_W + 2 * C_VHEADS
C_SPLITS = (C_QKV_W, C_QKV_W + C_V_W, C_QKV_W + C_V_W + C_VHEADS)
C_CONV = 4
C_CHUNK = 64
C_NORM_EPS = 1e-6

N_EXPERTS = 64
TOP_K = 8
N_GROUPS = 8
TOPK_GROUPS = 4
D_EXPERT = D_MODEL // 4
ROUTE_SCALE = 2.5
MOE_BLOCK = 128

kernel_name = 'hybrid_diffattn_rwkv7_gdn_moe'


def layer_norm(x, g, b):
    xf = x.astype(F32)
    mu = jnp.mean(xf, -1, keepdims=True)
    var = jnp.mean(jnp.square(xf - mu), -1, keepdims=True)
    return ((xf - mu) * lax.rsqrt(var + LN_EPS) * g + b).astype(x.dtype)


def rms_norm(x, g, eps):
    xf = x.astype(F32)
    return xf * lax.rsqrt(jnp.mean(jnp.square(xf), -1, keepdims=True) + eps) * g


def l2_normalize(x, eps=1e-6):
    xf = x.astype(F32)
    return xf * lax.rsqrt(jnp.sum(jnp.square(xf), -1, keepdims=True) + eps)


def token_shift(p):
    return jnp.pad(p, ((0, 0), (1, 0), (0, 0)))[:, :-1]


def alibi_slopes(n):
    return jnp.asarray(np.array([2.0 ** (-8.0 * (i + 1) / n) for i in range(n)], dtype=np.float32))


def swiglu(h, wg, wu, wd):
    return (jax.nn.silu(h @ wg) * (h @ wu)) @ wd


def causal_depthwise_conv(x, w):
    kw, ch = w.shape
    return lax.conv_general_dilated(x, w[:, None, :].astype(x.dtype), window_strides=(1,), padding=[(kw - 1, 0)], dimension_numbers=('NWC', 'WIO', 'NWC'), feature_group_count=ch)


def blocked_diff_softmax(q, k, v, lam, slopes):
    bn, s = q.shape[0], q.shape[1]
    nb = s // A_QBLOCK
    scale = A_QKDIM ** -0.5
    pos_k = jnp.arange(s)
    qb = jnp.moveaxis(q.reshape(bn, nb, A_QBLOCK, A_HEADS, 2, A_QKDIM), 1, 0)

    def one_block(args):
        q_blk, i = args
        pos_q = i * A_QBLOCK + jnp.arange(A_QBLOCK)
        dist = (pos_q[:, None] - pos_k[None, :]).astype(F32)
        sc = jnp.einsum('bqhcd,bkhcd->bhcqk', q_blk, k).astype(F32) * scale
        sc = sc - slopes[None, :, None, None, None] * dist
        sc = jnp.where(dist >= 0, sc, -jnp.inf)
        p = jax.nn.softmax(sc, axis=-1)
        attn = p[:, :, 0] - lam * p[:, :, 1]
        return jnp.einsum('bhqk,bkhd->bqhd', attn.astype(v.dtype), v)

    out = lax.map(one_block, (qb, jnp.arange(nb)))
    return jnp.moveaxis(out, 0, 1).reshape(bn, s, A_HEADS, A_VDIM)


def diff_attention_group(pa, lam_params, norm_g, layer_idx):
    bn, s, _ = pa.shape
    q, k, v = jnp.split(pa, 3, axis=-1)
    q = q.reshape(bn, s, A_HEADS, 2, A_QKDIM)
    k = k.reshape(bn, s, A_HEADS, 2, A_QKDIM)
    v = v.reshape(bn, s, A_HEADS, A_VDIM)
    lam_init = 0.8 - 0.6 * math.exp(-0.3 * layer_idx)
    lp = lam_params.astype(F32)
    lam = jnp.exp(jnp.sum(lp[0] * lp[1])) - jnp.exp(jnp.sum(lp[2] * lp[3])) + lam_init
    o = blocked_diff_softmax(q, k, v, lam, alibi_slopes(A_HEADS))
    o = rms_norm(o, norm_g, A_NORM_EPS) * (1.0 - lam_init)
    return o.reshape(bn, s, A_WIDTH).astype(pa.dtype)


def rwkv7_scan(r, w_log, k, v, kk, a):
    bn, s, h, n = r.shape
    decay = jnp.exp(-jnp.exp(w_log.astype(F32)))
    xs = tuple(jnp.moveaxis(t.astype(F32), 1, 0) for t in (r, decay, k, v, kk, a))

    def step(state, inp):
        r_t, d_t, k_t, v_t, kk_t, a_t = inp
        sa = jnp.einsum('bhvk,bhk->bhv', state, -kk_t)
        state = state * d_t[:, :, None, :] + sa[..., None] * (kk_t * a_t)[:, :, None, :] + v_t[..., None] * k_t[:, :, None, :]
        return state, jnp.einsum('bhvk,bhk->bhv', state, r_t)

    _, ys = lax.scan(step, jnp.zeros((bn, h, n, n), F32), xs)
    return jnp.moveaxis(ys, 0, 1)


def rwkv7_group(pb, mix, w0, w2, a0, a2, g2, k_k, k_a, r_k, lnx_g, lnx_b):
    bn, s, _ = pb.shape
    pb = pb + (token_shift(pb) - pb) * mix
    r, k, v, xw, xa, xg = jnp.split(pb, B_SPLITS, axis=-1)
    w_log = -jax.nn.softplus(-(w0 + jnp.tanh(xw) @ w2)) - 0.5
    a = jax.nn.sigmoid(a0 + xa @ a2)
    g = jax.nn.sigmoid(xg) @ g2

    def hd(t):
        return t.reshape(bn, s, B_HEADS, B_HEAD)

    def hp(t):
        return t.reshape(B_HEADS, B_HEAD)

    r, k, v, w_log, a = hd(r), hd(k), hd(v), hd(w_log), hd(a)
    kk = l2_normalize(k * hp(k_k))
    k = k * (1.0 + (a - 1.0) * hp(k_a))
    y = rwkv7_scan(r, w_log, k, v, kk, a)
    mu = jnp.mean(y, -1, keepdims=True)
    var = jnp.mean(jnp.square(y - mu), -1, keepdims=True)
    y = (y - mu) * lax.rsqrt(var + B_LNX_EPS) * hp(lnx_g) + hp(lnx_b)
    y = y + jnp.sum(r * k * hp(r_k), -1, keepdims=True) * v
    return (y.reshape(bn, s, B_WIDTH) * g).astype(pb.dtype)


def attn_rwkv_mixer(x, w_in, a_lambda, a_norm_g, b_mix, b_w0, b_w2, b_a0, b_a2, b_g2, b_k_k, b_k_a, b_r_k, b_lnx_g, b_lnx_b, w_out, layer_idx):
    p = x @ w_in
    ya = diff_attention_group(p[..., :A_COLS], a_lambda, a_norm_g, layer_idx)
    yb = rwkv7_group(p[..., A_COLS:], b_mix, b_w0, b_w2, b_a0, b_a2, b_g2, b_k_k, b_k_a, b_r_k, b_lnx_g, b_lnx_b)
    return jnp.concatenate([ya, yb], axis=-1) @ w_out


def chunk_gated_delta_rule(q, k, v, g, beta):
    bn, s, h, dk = q.shape
    dv = v.shape[-1]
    n = s // C_CHUNK

    def chunks(t):
        return jnp.swapaxes(t.astype(F32).reshape((bn, n, C_CHUNK, h) + t.shape[3:]), 2, 3)

    q, k, v, g, beta = chunks(q), chunks(k), chunks(v), chunks(g), chunks(beta)
    g = jnp.cumsum(g, axis=-1)
    tri = jnp.tril(jnp.ones((C_CHUNK, C_CHUNK), bool))
    strict = jnp.tril(jnp.ones((C_CHUNK, C_CHUNK), bool), -1)
    decay = jnp.exp(jnp.where(tri, g[..., :, None] - g[..., None, :], -jnp.inf))
    kb = k * beta[..., None]
    lower = jnp.where(strict, jnp.einsum('bnhid,bnhjd->bnhij', kb, k) * decay, 0.0)
    rhs = jnp.concatenate([v * beta[..., None], kb * jnp.exp(g)[..., None]], axis=-1)
    sol = lax.linalg.triangular_solve(jnp.eye(C_CHUNK, dtype=F32) + lower, rhs, left_side=True, lower=True, unit_diagonal=True)
    u, w = sol[..., :dv], sol[..., dv:]
    attn = jnp.where(tri, jnp.einsum('bnhid,bnhjd->bnhij', q, k) * decay, 0.0)
    q_dec = q * jnp.exp(g)[..., None]
    k_dec = k * jnp.exp(g[..., -1:] - g)[..., None]
    g_last = jnp.exp(g[..., -1])

    def step(state, inp):
        u_c, w_c, q_c, k_c, a_c, gl = inp
        v_new = u_c - jnp.einsum('bhcd,bhde->bhce', w_c, state)
        o = jnp.einsum('bhcd,bhde->bhce', q_c, state) + jnp.einsum('bhij,bhje->bhie', a_c, v_new)
        state = state * gl[..., None, None] + jnp.einsum('bhcd,bhce->bhde', k_c, v_new)
        return state, o

    xs = tuple(jnp.moveaxis(t, 1, 0) for t in (u, w, q_dec, k_dec, attn, g_last))
    _, o = lax.scan(step, jnp.zeros((bn, h, dk, dv), F32), xs)
    return o.transpose(1, 0, 3, 2, 4).reshape(bn, s, h, dv)


def gated_deltanet_mixer(x, w_in, conv_w, a_log, dt_bias, norm_g, w_out):
    bn, s, _ = x.shape
    p = x @ w_in
    qkv, z, b, a = jnp.split(p, C_SPLITS, axis=-1)
    qkv = jax.nn.silu(causal_depthwise_conv(qkv, conv_w))
    q, k, v = jnp.split(qkv, (C_QK_W, 2 * C_QK_W), axis=-1)
    rep = C_VHEADS // C_KHEADS
    q = jnp.repeat(l2_normalize(q.reshape(bn, s, C_KHEADS, C_DK)) * (C_DK ** -0.5), rep, axis=2)
    k = jnp.repeat(l2_normalize(k.reshape(bn, s, C_KHEADS, C_DK)), rep, axis=2)
    v = v.reshape(bn, s, C_VHEADS, C_DV)
    beta = jax.nn.sigmoid(b.astype(F32))
    g = -jnp.exp(a_log.astype(F32)) * jax.nn.softplus(a.astype(F32) + dt_bias)
    o = chunk_gated_delta_rule(q, k, v, g, beta)
    o = rms_norm(o, norm_g, C_NORM_EPS) * jax.nn.silu(z.reshape(bn, s, C_VHEADS, C_DV).astype(F32))
    return o.reshape(bn, s, C_V_W).astype(x.dtype) @ w_out


def moe_ffn(x, router_w, router_b, exp_gate, exp_up, exp_down, sh_gate, sh_up, sh_down):
    bn, s, d = x.shape
    t = bn * s
    h = x.reshape(t, d)
    scores = jax.nn.sigmoid((h @ router_w).astype(F32))
    choice = scores + router_b
    per_group = N_EXPERTS // N_GROUPS
    grp_score = jnp.sum(lax.top_k(choice.reshape(t, N_GROUPS, per_group), 2)[0], axis=-1)
    _, gidx = lax.top_k(grp_score, TOPK_GROUPS)
    gmask = jnp.any(gidx[..., None] == jnp.arange(N_GROUPS), axis=1)
    choice = jnp.where(jnp.repeat(gmask, per_group, axis=1), choice, -jnp.inf)
    _, eidx = lax.top_k(choice, TOP_K)
    gate = jnp.take_along_axis(scores, eidx, axis=1)
    gate = gate / jnp.sum(gate, -1, keepdims=True) * ROUTE_SCALE

    tk = t * TOP_K
    flat_e = eidx.reshape(tk)
    flat_tok = jnp.repeat(jnp.arange(t, dtype=jnp.int32), TOP_K)
    flat_g = gate.reshape(tk)
    order = jnp.argsort(flat_e)
    e_sorted = flat_e[order]
    counts = jnp.bincount(flat_e, length=N_EXPERTS)
    start = jnp.cumsum(counts) - counts
    padded = (counts + MOE_BLOCK - 1) // MOE_BLOCK * MOE_BLOCK
    pend = jnp.cumsum(padded)
    pstart = pend - padded
    dest = pstart[e_sorted] + jnp.arange(tk, dtype=jnp.int32) - start[e_sorted]
    n_blocks = (tk + N_EXPERTS * (MOE_BLOCK - 1) + MOE_BLOCK - 1) // MOE_BLOCK
    npad = n_blocks * MOE_BLOCK
    slot_tok = jnp.full((npad,), t, jnp.int32).at[dest].set(flat_tok[order])
    slot_gate = jnp.zeros((npad,), F32).at[dest].set(flat_g[order])
    block_e = jnp.minimum(jnp.searchsorted(pend, jnp.arange(n_blocks) * MOE_BLOCK, side='right'), N_EXPERTS - 1)
    h_pad = jnp.concatenate([h, jnp.zeros((1, d), h.dtype)], axis=0)

    def expert_block(args):
        tok, e = args
        return swiglu(h_pad[tok], exp_gate[e], exp_up[e], exp_down[e])

    yb = lax.map(expert_block, (slot_tok.reshape(n_blocks, MOE_BLOCK), block_e)).reshape(npad, d)
    routed = jnp.zeros((t + 1, d), F32).at[slot_tok].add(yb.astype(F32) * slot_gate[:, None])[:t]
    shared = swiglu(h, sh_gate, sh_up, sh_down)
    return (routed.astype(x.dtype) + shared).reshape(bn, s, d)


def setup_inputs(seed: int = 0) -> dict:
    key = jax.random.key(seed)
    ks = iter(jax.random.split(key, 96))

    def nrm(shape, scale):
        return jax.random.normal(next(ks), shape, F32) * scale

    def unif(shape, lo, hi):
        return jax.random.uniform(next(ks), shape, F32, lo, hi)

    def gain(n):
        return 1.0 + nrm((n,), 0.02)

    def bias(n):
        return nrm((n,), 0.02)

    def moe_params(pfx):
        out = {}
        out[pfx + 'router_w'] = nrm((D_MODEL, N_EXPERTS), D_MODEL ** -0.5)
        out[pfx + 'router_b'] = nrm((N_EXPERTS,), 0.01)
        out[pfx + 'exp_gate'] = nrm((N_EXPERTS, D_MODEL, D_EXPERT), D_MODEL ** -0.5)
        out[pfx + 'exp_up'] = nrm((N_EXPERTS, D_MODEL, D_EXPERT), D_MODEL ** -0.5)
        out[pfx + 'exp_down'] = nrm((N_EXPERTS, D_EXPERT, D_MODEL), DN_BETA * D_EXPERT ** -0.5)
        out[pfx + 'sh_gate'] = nrm((D_MODEL, D_EXPERT), D_MODEL ** -0.5)
        out[pfx + 'sh_up'] = nrm((D_MODEL, D_EXPERT), D_MODEL ** -0.5)
        out[pfx + 'sh_down'] = nrm((D_EXPERT, D_MODEL), DN_BETA * D_EXPERT ** -0.5)
        return out

    inp = {}
    inp['x'] = nrm((BATCH, SEQ, D_MODEL), 1.0)
    inp['l0_w_in'] = nrm((D_MODEL, A_COLS + B_COLS), D_MODEL ** -0.5)
    inp['l0_a_lambda'] = nrm((4, A_QKDIM), 0.1)
    inp['l0_a_norm_g'] = gain(A_VDIM)
    inp['l0_b_mix'] = unif((B_COLS,), 0.0, 1.0)
    inp['l0_b_w0'] = unif((B_WIDTH,), -2.0, 1.0)
    inp['l0_b_w2'] = nrm((B_LORA_W, B_WIDTH), B_LORA_W ** -0.5)
    inp['l0_b_a0'] = unif((B_WIDTH,), -0.5, 0.5)
    inp['l0_b_a2'] = nrm((B_LORA_A, B_WIDTH), B_LORA_A ** -0.5)
    inp['l0_b_g2'] = nrm((B_LORA_G, B_WIDTH), B_LORA_G ** -0.5)
    inp['l0_b_k_k'] = 0.85 + nrm((B_WIDTH,), 0.05)
    inp['l0_b_k_a'] = 1.0 + nrm((B_WIDTH,), 0.05)
    inp['l0_b_r_k'] = nrm((B_WIDTH,), 0.1)
    inp['l0_b_lnx_g'] = gain(B_WIDTH)
    inp['l0_b_lnx_b'] = bias(B_WIDTH)
    inp['l0_w_out'] = nrm((A_WIDTH + B_WIDTH, D_MODEL), DN_BETA * (A_WIDTH + B_WIDTH) ** -0.5)
    inp['l0_ln1_g'] = gain(D_MODEL)
    inp['l0_ln1_b'] = bias(D_MODEL)
    inp.update(moe_params('l0_'))
    inp['l0_ln2_g'] = gain(D_MODEL)
    inp['l0_ln2_b'] = bias(D_MODEL)
    inp['l1_w_in'] = nrm((D_MODEL, C_COLS), D_MODEL ** -0.5)
    inp['l1_conv_w'] = nrm((C_CONV, C_QKV_W), C_CONV ** -0.5)
    inp['l1_a_log'] = jnp.log(unif((C_VHEADS,), 1.0, 16.0))
    dt = jnp.exp(unif((C_VHEADS,), math.log(1e-3), math.log(1e-1)))
    inp['l1_dt_bias'] = dt + jnp.log(-jnp.expm1(-dt))
    inp['l1_norm_g'] = gain(C_DV)
    inp['l1_w_out'] = nrm((C_V_W, D_MODEL), DN_BETA * C_V_W ** -0.5)
    inp['l1_ln1_g'] = gain(D_MODEL)
    inp['l1_ln1_b'] = bias(D_MODEL)
    inp.update(moe_params('l1_'))
    inp['l1_ln2_g'] = gain(D_MODEL)
    inp['l1_ln2_b'] = bias(D_MODEL)
    return inp


def reference(x, l0_w_in, l0_a_lambda, l0_a_norm_g, l0_b_mix, l0_b_w0, l0_b_w2, l0_b_a0, l0_b_a2, l0_b_g2, l0_b_k_k, l0_b_k_a, l0_b_r_k, l0_b_lnx_g, l0_b_lnx_b, l0_w_out, l0_ln1_g, l0_ln1_b, l0_router_w, l0_router_b, l0_exp_gate, l0_exp_up, l0_exp_down, l0_sh_gate, l0_sh_up, l0_sh_down, l0_ln2_g, l0_ln2_b, l1_w_in, l1_conv_w, l1_a_log, l1_dt_bias, l1_norm_g, l1_w_out, l1_ln1_g, l1_ln1_b, l1_router_w, l1_router_b, l1_exp_gate, l1_exp_up, l1_exp_down, l1_sh_gate, l1_sh_up, l1_sh_down, l1_ln2_g, l1_ln2_b):
    mix_params = (
        (l0_w_in, l0_a_lambda, l0_a_norm_g, l0_b_mix, l0_b_w0, l0_b_w2, l0_b_a0, l0_b_a2, l0_b_g2, l0_b_k_k, l0_b_k_a, l0_b_r_k, l0_b_lnx_g, l0_b_lnx_b, l0_w_out),
        (l1_w_in, l1_conv_w, l1_a_log, l1_dt_bias, l1_norm_g, l1_w_out),
    )
    norm_params = ((l0_ln1_g, l0_ln1_b, l0_ln2_g, l0_ln2_b), (l1_ln1_g, l1_ln1_b, l1_ln2_g, l1_ln2_b))
    moe_params = (
        (l0_router_w, l0_router_b, l0_exp_gate, l0_exp_up, l0_exp_down, l0_sh_gate, l0_sh_up, l0_sh_down),
        (l1_router_w, l1_router_b, l1_exp_gate, l1_exp_up, l1_exp_down, l1_sh_gate, l1_sh_up, l1_sh_down),
    )
    for layer in range(DEPTH):
        if layer % 2 == 0:
            m = attn_rwkv_mixer(x, *mix_params[layer], layer_idx=layer)
        else:
            m = gated_deltanet_mixer(x, *mix_params[layer])
        ln1_g, ln1_b, ln2_g, ln2_b = norm_params[layer]
        x = layer_norm(DN_ALPHA * x + m, ln1_g, ln1_b)
        x = layer_norm(DN_ALPHA * x + moe_ffn(x, *moe_params[layer]), ln2_g, ln2_b)
    return x
```

```python
import functools
import math

import jax
import jax.numpy as jnp
import numpy as np
from jax import lax
from jax.experimental import pallas as pl
from jax.experimental.pallas import tpu as pltpu

F32 = jnp.float32
BF16 = jnp.bfloat16
I32 = jnp.int32
HIGHEST = lax.Precision.HIGHEST

LANES = 128
VMEM_LIMIT = 56 * 1024 * 1024

DEPTH = 2
DN_ALPHA = (2.0 * DEPTH) ** 0.25
LN_EPS = 1e-5

A_HEADS = 4
A_VDIM = 128
A_QKDIM = 64
A_NORM_EPS = 1e-5

B_HEAD = 64
B_WIDTH = 512
B_LNX_EPS = 64e-5
B_CHUNK = 64

C_DK = 128
C_KHEADS = 8
C_VHEADS = 16
C_CONV = 4
C_CHUNK = 64
C_NORM_EPS = 1e-6

N_EXPERTS = 64
TOP_K = 8
N_GROUPS = 8
TOPK_GROUPS = 4
ROUTE_SCALE = 2.5
MOE_ROWS = 256


def _params(*sem):
    return pltpu.CompilerParams(dimension_semantics=sem, vmem_limit_bytes=VMEM_LIMIT)


def _dot(a, b, precision=None):
    return jnp.dot(a, b, preferred_element_type=F32, precision=precision)


def _dot_nt(a, b, precision=None):
    return lax.dot_general(a, b, (((1,), (1,)), ((), ())), preferred_element_type=F32, precision=precision)


def _dot_tn(a, b, precision=None):
    return lax.dot_general(a, b, (((0,), (0,)), ((), ())), preferred_element_type=F32, precision=precision)


def _sigmoid(x):
    return 1.0 / (1.0 + jnp.exp(-x))


def _silu(x):
    return x * _sigmoid(x)


def _softplus(x):
    return jnp.maximum(x, 0.0) + jnp.log(1.0 + jnp.exp(-jnp.abs(x)))


def _mm_body(x_ref, w_ref, o_ref):
    o_ref[...] = _dot(x_ref[...].astype(BF16), w_ref[...]).astype(o_ref.dtype)


def _matmul(x, w, *, tm=512, tn=512, out_dtype=F32):
    m, k = x.shape
    n = w.shape[1]
    tn = min(tn, n)
    assert m % tm == 0 and n % tn == 0
    return pl.pallas_call(
        _mm_body,
        out_shape=jax.ShapeDtypeStruct((m, n), out_dtype),
        grid=(m // tm, n // tn),
        in_specs=[pl.BlockSpec((tm, k), lambda i, j: (i, 0)),
                  pl.BlockSpec((k, tn), lambda i, j: (0, j))],
        out_specs=pl.BlockSpec((tm, tn), lambda i, j: (i, j)),
        compiler_params=_params("parallel", "arbitrary"),
        name="matmul",
    )(x, w.astype(BF16))


def _mm_nt_body(w_ref, x_ref, o_ref):
    o_ref[...] = _dot_nt(w_ref[...], x_ref[...].astype(BF16))


def _matmul_t(wt, x, *, tm=512):
    n, k = wt.shape
    m = x.shape[0]
    return pl.pallas_call(
        _mm_nt_body,
        out_shape=jax.ShapeDtypeStruct((n, m), F32),
        grid=(m // tm,),
        in_specs=[pl.BlockSpec((n, k), lambda i: (0, 0)),
                  pl.BlockSpec((tm, k), lambda i: (i, 0))],
        out_specs=pl.BlockSpec((n, tm), lambda i: (0, i)),
        compiler_params=_params("parallel"),
        name="matmul_t",
    )(wt.astype(BF16), x)


def _layer_norm_rows(y, g, b):
    mu = jnp.mean(y, axis=-1, keepdims=True)
    d = y - mu
    var = jnp.mean(d * d, axis=-1, keepdims=True)
    return d * lax.rsqrt(var + LN_EPS) * g + b


def _mm_ln_body(n_parts, *refs):
    parts = refs[:n_parts]
    ws = refs[n_parts:2 * n_parts]
    res_ref, g_ref, b_ref, o_ref = refs[2 * n_parts:]
    acc = DN_ALPHA * res_ref[...]
    for p_ref, w_ref in zip(parts, ws):
        acc = acc + _dot(p_ref[...].astype(BF16), w_ref[...])
    o_ref[...] = _layer_norm_rows(acc, g_ref[...], b_ref[...])


def _matmul_res_ln(parts, ws, res, g, b, *, tm=256):
    m, d = res.shape
    n_parts = len(parts)
    in_specs = [pl.BlockSpec((tm, p.shape[1]), lambda i: (i, 0)) for p in parts]
    in_specs += [pl.BlockSpec(w.shape, lambda i: (0, 0)) for w in ws]
    in_specs += [pl.BlockSpec((tm, d), lambda i: (i, 0)),
                 pl.BlockSpec((1, d), lambda i: (0, 0)),
                 pl.BlockSpec((1, d), lambda i: (0, 0))]
    return pl.pallas_call(
        functools.partial(_mm_ln_body, n_parts),
        out_shape=jax.ShapeDtypeStruct((m, d), F32),
        grid=(m // tm,),
        in_specs=in_specs,
        out_specs=pl.BlockSpec((tm, d), lambda i: (i, 0)),
        compiler_params=_params("parallel"),
        name="matmul_res_ln",
    )(*parts, *[w.astype(BF16) for w in ws], res, g.reshape(1, d), b.reshape(1, d))


def _attn_body(seq_blocks, tq, lam_init, slopes_ref, lam_ref, g_ref, q_ref, k_ref, v_ref, o_ref):
    h = pl.program_id(1)
    qi = pl.program_id(2)
    slope = slopes_ref[h]
    lp = lam_ref[...]
    lam = (jnp.exp(jnp.sum(lp[0:1] * lp[1:2], axis=1, keepdims=True))
           - jnp.exp(jnp.sum(lp[2:3] * lp[3:4], axis=1, keepdims=True)) + lam_init)
    lane = lax.broadcasted_iota(I32, (1, A_VDIM), 1)
    q = q_ref[...] * (A_QKDIM ** -0.5)
    q0 = jnp.where(lane < A_QKDIM, q, 0.0).astype(BF16)
    q1 = jnp.where(lane >= A_QKDIM, q, 0.0).astype(BF16)
    rel = (lax.broadcasted_iota(I32, (tq, tq), 0) - lax.broadcasted_iota(I32, (tq, tq), 1))

    def step(j, carry):
        m0, l0, a0, m1, l1, a1 = carry
        kb = k_ref[pl.ds(pl.multiple_of(j * tq, tq), tq), :].astype(BF16)
        vb = v_ref[pl.ds(pl.multiple_of(j * tq, tq), tq), :].astype(BF16)
        dist = rel + (qi - j) * tq
        bias = slope * dist.astype(F32)
        keep = dist >= 0

        def one(qc, m, l, a):
            s = jnp.where(keep, _dot_nt(qc, kb) - bias, -jnp.inf)
            m_new = jnp.maximum(m, jnp.max(s, axis=1, keepdims=True))
            alpha = jnp.exp(m - m_new)
            p = jnp.exp(s - m_new)
            l_new = alpha * l + jnp.sum(p, axis=1, keepdims=True)
            a_new = alpha * a + _dot(p.astype(BF16), vb)
            return m_new, l_new, a_new

        m0, l0, a0 = one(q0, m0, l0, a0)
        m1, l1, a1 = one(q1, m1, l1, a1)
        return m0, l0, a0, m1, l1, a1

    neg = jnp.full((tq, 1), -jnp.inf, F32)
    zero1 = jnp.zeros((tq, 1), F32)
    zero = jnp.zeros((tq, A_VDIM), F32)
    _, l0, a0, _, l1, a1 = lax.fori_loop(0, qi + 1, step, (neg, zero1, zero, neg, zero1, zero))
    o = a0 / l0 - lam * (a1 / l1)
    ms = jnp.mean(o * o, axis=1, keepdims=True)
    o_ref[...] = o * lax.rsqrt(ms + A_NORM_EPS) * g_ref[...] * (1.0 - lam_init)


def _diff_attention(pa, lam_params, norm_g, bn, s, layer_idx, *, tq=256):
    t = pa.shape[0]
    nq = s // tq
    lam_init = 0.8 - 0.6 * math.exp(-0.3 * layer_idx)
    slopes = jnp.asarray(np.array([2.0 ** (-8.0 * (i + 1) / A_HEADS) for i in range(A_HEADS)], dtype=np.float32))
    body = functools.partial(_attn_body, nq, tq, lam_init)
    return pl.pallas_call(
        body,
        out_shape=jax.ShapeDtypeStruct((t, A_HEADS * A_VDIM), F32),
        grid=(bn, A_HEADS, nq),
        in_specs=[pl.BlockSpec(memory_space=pltpu.SMEM),
                  pl.BlockSpec((4, A_QKDIM), lambda b, h, i: (0, 0)),
                  pl.BlockSpec((1, A_VDIM), lambda b, h, i: (0, 0)),
                  pl.BlockSpec((tq, A_VDIM), lambda b, h, i: (b * nq + i, h)),
                  pl.BlockSpec((s, A_VDIM), lambda b, h, i: (b, A_HEADS + h)),
                  pl.BlockSpec((s, A_VDIM), lambda b, h, i: (b, 2 * A_HEADS + h))],
        out_specs=pl.BlockSpec((tq, A_VDIM), lambda b, h, i: (b * nq + i, h)),
        compiler_params=_params("parallel", "parallel", "arbitrary"),
        name="diff_attention",
    )(slopes, lam_params.astype(F32), norm_g.reshape(1, A_VDIM), pa, pa, pa)


def _tri_inv(a, blk):
    n = a.shape[0]
    row = lax.broadcasted_iota(I32, (n, n), 0)
    col = lax.broadcasted_iota(I32, (n, n), 1)
    eye = (row == col).astype(F32)

    def same(m):
        return (row // m) == (col // m)

    d = jnp.where(same(8), a, 0.0)
    d2 = _dot(d, d, HIGHEST)
    d4 = _dot(d2, d2, HIGHEST)
    t = _dot(_dot(eye - d, eye + d2, HIGHEST), eye + d4, HIGHEST)
    m = 8
    while m < blk:
        e = jnp.where(same(2 * m) & jnp.logical_not(same(m)), a, 0.0)
        t = t - _dot(t, _dot(e, t, HIGHEST), HIGHEST)
        m *= 2
    return t


def _group_sum(x, ones_blk):
    return _dot(x, ones_blk, HIGHEST)


def _rwkv_pre_body(ts, pb_ref, mix_ref, w0_ref, w2_ref, a0_ref, a2_ref, g2_ref, kk_ref, ka_ref, rk_ref, ones_ref,
                   r_out, k_out, v_out, kap_out, b_out, ld_out, g_out, bv_out, carry):
    i = pl.program_id(1)

    @pl.when(i == 0)
    def _():
        carry[...] = jnp.zeros_like(carry)

    cur = pb_ref[...]
    rolled = pltpu.roll(cur, 1, 0)
    first = lax.broadcasted_iota(I32, (ts, 1), 0) == 0
    prev = jnp.where(first, carry[0:1, :], rolled)
    carry[0:1, :] = cur[ts - 1:ts, :]
    x = cur + (prev - cur) * mix_ref[...]
    r = x[:, 0:B_WIDTH]
    k = x[:, B_WIDTH:2 * B_WIDTH]
    v = x[:, 2 * B_WIDTH:3 * B_WIDTH]
    xwa = x[:, 3 * B_WIDTH:3 * B_WIDTH + LANES]
    xg = x[:, 3 * B_WIDTH + LANES:3 * B_WIDTH + 2 * LANES]
    w_log = -_softplus(-(w0_ref[...] + _dot(jnp.tanh(xwa).astype(BF16), w2_ref[...]))) - 0.5
    a = _sigmoid(a0_ref[...] + _dot(xwa.astype(BF16), a2_ref[...]))
    g = _dot(_sigmoid(xg).astype(BF16), g2_ref[...])
    ones_blk = ones_ref[...]
    kkr = k * kk_ref[...]
    kp = k * (1.0 + (a - 1.0) * ka_ref[...])
    rkr = r * kp * rk_ref[...]
    for c in range(B_WIDTH // LANES):
        sl = slice(c * LANES, (c + 1) * LANES)
        kn = kkr[:, sl]
        kap = kn * lax.rsqrt(_group_sum(kn * kn, ones_blk) + 1e-6)
        kap_out[:, sl] = kap
        b_out[:, sl] = kap * a[:, sl]
        bv_out[:, sl] = _group_sum(rkr[:, sl], ones_blk) * v[:, sl]
    r_out[...] = r
    k_out[...] = kp
    v_out[...] = v
    ld_out[...] = -jnp.exp(w_log)
    g_out[...] = g


def _head_group_ones():
    idx = np.arange(LANES) // B_HEAD
    return jnp.asarray((idx[:, None] == idx[None, :]).astype(np.float32))


def _rwkv_pre(pb, mix, w0, w2, a0, a2, g2, k_k, k_a, r_k, bn, s, *, ts=256):
    t, cols = pb.shape
    nt = s // ts
    lora = w2.shape[0]
    w2p = jnp.concatenate([w2, jnp.zeros_like(w2)], axis=0).astype(BF16)
    a2p = jnp.concatenate([jnp.zeros_like(a2), a2], axis=0).astype(BF16)
    assert lora * 2 == LANES
    row = lambda z: z.reshape(1, -1)
    full = lambda shape: pl.BlockSpec(shape, lambda b, i: (0, 0))
    out = jax.ShapeDtypeStruct((t, B_WIDTH), F32)
    tile = pl.BlockSpec((ts, B_WIDTH), lambda b, i: (b * nt + i, 0))
    return pl.pallas_call(
        functools.partial(_rwkv_pre_body, ts),
        out_shape=[out] * 8,
        grid=(bn, nt),
        in_specs=[pl.BlockSpec((ts, cols), lambda b, i: (b * nt + i, 0)),
                  full((1, cols)), full((1, B_WIDTH)), full((LANES, B_WIDTH)), full((1, B_WIDTH)),
                  full((LANES, B_WIDTH)), full((LANES, B_WIDTH)), full((1, B_WIDTH)), full((1, B_WIDTH)),
                  full((1, B_WIDTH)), full((LANES, LANES))],
        out_specs=[tile] * 8,
        scratch_shapes=[pltpu.VMEM((8, cols), F32)],
        compiler_params=_params("parallel", "arbitrary"),
        name="rwkv_pre",
    )(pb, row(mix), row(w0), w2p, row(a0), a2p, g2.astype(BF16), row(k_k), row(k_a), row(r_k), _head_group_ones())


def _rwkv_scan_body(c, r_ref, k_ref, v_ref, kap_ref, b_ref, ld_ref, g_ref, bv_ref, lng_ref, lnb_ref, ones_ref,
                    o_ref, state):
    @pl.when(pl.program_id(1) == 0)
    def _():
        state[...] = jnp.zeros_like(state)

    n = 2 * c
    row = lax.broadcasted_iota(I32, (n, n), 0)
    col = lax.broadcasted_iota(I32, (n, n), 1)
    same = (row // c) == (col // c)
    strict = same & (row > col)
    incl = same & (row >= col)
    trow = lax.broadcasted_iota(I32, (c, c), 0)
    tcol = lax.broadcasted_iota(I32, (c, c), 1)
    tril = (trow >= tcol).astype(F32)
    head0 = lax.broadcasted_iota(I32, (1, LANES), 1) < B_HEAD
    ones_blk = ones_ref[...]

    def sel(z):
        return jnp.where(head0, z[0:c], z[c:n])

    def dup(z):
        return jnp.concatenate([z, z], axis=0)

    for p in range(B_WIDTH // LANES):
        sl = slice(p * LANES, (p + 1) * LANES)
        ld = ld_ref[:, sl]
        r, k, v, kap, b = r_ref[:, sl], k_ref[:, sl], v_ref[:, sl], kap_ref[:, sl], b_ref[:, sl]
        cum = _dot(tril, ld, HIGHEST)
        gam = jnp.exp(cum)
        einv = jnp.exp(-cum)
        kt = kap * jnp.exp(cum - ld)
        rt = r * gam
        kh = k * einv
        bh = b * einv
        g_last = gam[c - 1:c, :]
        x = jnp.concatenate([jnp.where(head0, kt, 0.0), jnp.where(head0, 0.0, kt),
                             jnp.where(head0, rt, 0.0), jnp.where(head0, 0.0, rt)], axis=0)
        y = jnp.concatenate([kh, kh, bh, bh], axis=0)
        m2 = _dot_nt(x, y, HIGHEST)
        a_kk = jnp.where(strict, m2[0:n, 0:n], 0.0)
        a_bk = jnp.where(strict, m2[0:n, n:2 * n], 0.0)
        b_rk = jnp.where(incl, m2[n:2 * n, 0:n], 0.0)
        b_rb = jnp.where(incl, m2[n:2 * n, n:2 * n], 0.0)
        tinv = _tri_inv(a_bk, c)
        s_old = state[p]
        ps = _dot_nt(jnp.concatenate([kt, rt], axis=0), s_old, HIGHEST)
        vv = dup(v)
        rhs = ps[0:c] + sel(_dot(a_kk, vv, HIGHEST))
        u = sel(_dot(tinv, dup(rhs), HIGHEST))
        yc = ps[c:n] + sel(_dot(b_rk, vv, HIGHEST) - _dot(b_rb, dup(u), HIGHEST))
        upd = _dot_tn(jnp.concatenate([v, u], axis=0),
                      jnp.concatenate([kh * g_last, -(bh * g_last)], axis=0), HIGHEST)
        state[p] = s_old * g_last + jnp.where(same, upd, 0.0)
        mu = _group_sum(yc, ones_blk) * (1.0 / B_HEAD)
        dy = yc - mu
        var = _group_sum(dy * dy, ones_blk) * (1.0 / B_HEAD)
        yn = dy * lax.rsqrt(var + B_LNX_EPS) * lng_ref[:, sl] + lnb_ref[:, sl]
        o_ref[:, sl] = (yn + bv_ref[:, sl]) * g_ref[:, sl]


def _rwkv_scan(r, k, v, kap, b, ld, g, bv, lnx_g, lnx_b, bn, s, *, c=B_CHUNK):
    t = r.shape[0]
    nc = s // c
    tile = pl.BlockSpec((c, B_WIDTH), lambda bi, i: (bi * nc + i, 0))
    vec = pl.BlockSpec((1, B_WIDTH), lambda bi, i: (0, 0))
    return pl.pallas_call(
        functools.partial(_rwkv_scan_body, c),
        out_shape=jax.ShapeDtypeStruct((t, B_WIDTH), F32),
        grid=(bn, nc),
        in_specs=[tile] * 8 + [vec, vec, pl.BlockSpec((LANES, LANES), lambda bi, i: (0, 0))],
        out_specs=tile,
        scratch_shapes=[pltpu.VMEM((B_WIDTH // LANES, LANES, LANES), F32)],
        compiler_params=_params("parallel", "arbitrary"),
        name="rwkv_scan",
    )(r, k, v, kap, b, ld, g, bv, lnx_g.reshape(1, -1), lnx_b.reshape(1, -1), _head_group_ones())


def _first_index_of_max(vals, idx, sentinel):
    m = jnp.max(vals, axis=0, keepdims=True)
    first = jnp.min(jnp.where(vals == m, idx, sentinel), axis=0, keepdims=True)
    return m, first


def _router_body(tm, w_ref, bias_ref, h_ref, eidx_ref, gate_ref):
    per_group = N_EXPERTS // N_GROUPS
    logits = _dot_nt(w_ref[...], h_ref[...], HIGHEST)
    scores = _sigmoid(logits)
    choice = scores + bias_ref[...]
    sub = lax.broadcasted_iota(I32, (per_group, tm), 0)
    grp_rows = []
    for g in range(N_GROUPS):
        cg = choice[g * per_group:(g + 1) * per_group, :]
        m1, first = _first_index_of_max(cg, sub, per_group)
        m2 = jnp.max(jnp.where(sub == first, -jnp.inf, cg), axis=0, keepdims=True)
        grp_rows.append(m1 + m2)
    gs = jnp.concatenate(grp_rows, axis=0)
    gidx = lax.broadcasted_iota(I32, (N_GROUPS, tm), 0)
    gsel = jnp.zeros((N_GROUPS, tm), jnp.bool_)
    for _ in range(TOPK_GROUPS):
        _, first = _first_index_of_max(gs, gidx, N_GROUPS)
        pick = gidx == first
        gsel = gsel | pick
        gs = jnp.where(pick, -jnp.inf, gs)
    ch = jnp.concatenate(
        [jnp.where(gsel[g:g + 1, :], choice[g * per_group:(g + 1) * per_group, :], -jnp.inf)
         for g in range(N_GROUPS)], axis=0)
    eidx = lax.broadcasted_iota(I32, (N_EXPERTS, tm), 0)
    idx_rows, score_rows = [], []
    for _ in range(TOP_K):
        _, first = _first_index_of_max(ch, eidx, N_EXPERTS)
        pick = eidx == first
        idx_rows.append(first)
        score_rows.append(jnp.sum(jnp.where(pick, scores, 0.0), axis=0, keepdims=True))
        ch = jnp.where(pick, -jnp.inf, ch)
    sc = jnp.concatenate(score_rows, axis=0)
    eidx_ref[...] = jnp.concatenate(idx_rows, axis=0)
    gate_ref[...] = sc / jnp.sum(sc, axis=0, keepdims=True) * ROUTE_SCALE


def _router(h, router_w, router_b, *, tm=512):
    t, d = h.shape
    return pl.pallas_call(
        functools.partial(_router_body, tm),
        out_shape=[jax.ShapeDtypeStruct((TOP_K, t), I32), jax.ShapeDtypeStruct((TOP_K, t), F32)],
        grid=(t // tm,),
        in_specs=[pl.BlockSpec((N_EXPERTS, d), lambda i: (0, 0)),
                  pl.BlockSpec((N_EXPERTS, 1), lambda i: (0, 0)),
                  pl.BlockSpec((tm, d), lambda i: (i, 0))],
        out_specs=[pl.BlockSpec((TOP_K, tm), lambda i: (0, i)), pl.BlockSpec((TOP_K, tm), lambda i: (0, i))],
        compiler_params=_params("parallel"),
        name="moe_router",
    )(router_w.T, router_b.reshape(N_EXPERTS, 1), h)


def _dispatch_plan(eidx_t, gate_t):
    t = eidx_t.shape[1]
    tk = t * TOP_K
    flat_e = eidx_t.T.reshape(tk)
    flat_g = gate_t.T.reshape(tk)
    order = jnp.argsort(flat_e)
    e_sorted = flat_e[order]
    counts = jnp.bincount(flat_e, length=N_EXPERTS)
    start = jnp.cumsum(counts) - counts
    padded = (counts + MOE_ROWS - 1) // MOE_ROWS * MOE_ROWS
    pend = jnp.cumsum(padded)
    pstart = pend - padded
    dest_sorted = (pstart[e_sorted] + jnp.arange(tk, dtype=I32) - start[e_sorted]).astype(I32)
    n_blocks = (tk + N_EXPERTS * (MOE_ROWS - 1) + MOE_ROWS - 1) // MOE_ROWS
    npad = n_blocks * MOE_ROWS
    slot_tok = jnp.zeros((npad,), I32).at[dest_sorted].set((order // TOP_K).astype(I32))
    slot_gate = jnp.zeros((npad,), F32).at[dest_sorted].set(flat_g[order])
    block_e = jnp.minimum(jnp.searchsorted(pend, jnp.arange(n_blocks) * MOE_ROWS, side='right'),
                          N_EXPERTS - 1).astype(I32)
    dest = jnp.zeros((tk,), I32).at[order].set(dest_sorted)
    return slot_tok, slot_gate, block_e, dest, n_blocks


def _gather_rows(src_hbm, idx_ref, n_rows, dst, sem):
    def issue(r, _):
        pltpu.make_async_copy(src_hbm.at[pl.ds(idx_ref[0, 0, r], 1), :], dst.at[pl.ds(r, 1), :], sem).start()
        return 0
    lax.fori_loop(0, n_rows, issue, 0)


def _wait_rows(src_hbm, n_rows, dst, sem):
    def wait(r, _):
        pltpu.make_async_copy(src_hbm.at[pl.ds(0, 1), :], dst.at[pl.ds(r, 1), :], sem).wait()
        return 0
    lax.fori_loop(0, n_rows, wait, 0)


def _expert_body(n_blocks, be_ref, cur_idx, nxt_idx, gate_ref, h_hbm, wg_ref, wu_ref, wd_ref, o_ref, xbuf, sem):
    i = pl.program_id(0)
    slot = i % 2

    @pl.when(i == 0)
    def _():
        _gather_rows(h_hbm, cur_idx, MOE_ROWS, xbuf.at[0], sem.at[0])

    @pl.when(i + 1 < n_blocks)
    def _():
        _gather_rows(h_hbm, nxt_idx, MOE_ROWS, xbuf.at[1 - slot], sem.at[1 - slot])

    _wait_rows(h_hbm, MOE_ROWS, xbuf.at[slot], sem.at[slot])
    x = xbuf[slot].astype(BF16)
    hid = _silu(_dot(x, wg_ref[...])) * _dot(x, wu_ref[...])
    y = _dot(hid.astype(BF16), wd_ref[...])
    rr = lax.broadcasted_iota(I32, (MOE_ROWS, MOE_ROWS), 0)
    cc = lax.broadcasted_iota(I32, (MOE_ROWS, MOE_ROWS), 1)
    gate_col = jnp.sum(jnp.where(rr == cc, gate_ref[0], 0.0), axis=1, keepdims=True)
    o_ref[...] = y * gate_col


def _expert_blocks(h, slot_tok, slot_gate, block_e, n_blocks, wg, wu, wd):
    t, d = h.shape
    de = wg.shape[2]
    npad = n_blocks * MOE_ROWS
    idx3 = slot_tok.reshape(n_blocks, 1, MOE_ROWS)
    gate3 = slot_gate.reshape(n_blocks, 1, MOE_ROWS)
    grid_spec = pltpu.PrefetchScalarGridSpec(
        num_scalar_prefetch=1,
        grid=(n_blocks,),
        in_specs=[pl.BlockSpec((1, 1, MOE_ROWS), lambda i, be: (i, 0, 0), memory_space=pltpu.SMEM),
                  pl.BlockSpec((1, 1, MOE_ROWS), lambda i, be: (jnp.minimum(i + 1, n_blocks - 1), 0, 0),
                               memory_space=pltpu.SMEM),
                  pl.BlockSpec((1, 1, MOE_ROWS), lambda i, be: (i, 0, 0)),
                  pl.BlockSpec(memory_space=pl.ANY),
                  pl.BlockSpec((None, d, de), lambda i, be: (be[i], 0, 0)),
                  pl.BlockSpec((None, d, de), lambda i, be: (be[i], 0, 0)),
                  pl.BlockSpec((None, de, d), lambda i, be: (be[i], 0, 0))],
        out_specs=pl.BlockSpec((MOE_ROWS, d), lambda i, be: (i, 0)),
        scratch_shapes=[pltpu.VMEM((2, MOE_ROWS, d), F32), pltpu.SemaphoreType.DMA((2,))],
    )
    return pl.pallas_call(
        functools.partial(_expert_body, n_blocks),
        out_shape=jax.ShapeDtypeStruct((npad, d), F32),
        grid_spec=grid_spec,
        compiler_params=_params("arbitrary"),
        name="moe_experts",
    )(block_e, idx3, idx3, gate3, h, wg.astype(BF16), wu.astype(BF16), wd.astype(BF16))


def _combine_body(n_tiles, tm, cur_idx, nxt_idx, y_hbm, h_ref, sg_ref, su_ref, sd_ref, g_ref, b_ref, o_ref, buf, sem):
    i = pl.program_id(0)
    slot = i % 2
    n_rows = tm * TOP_K

    @pl.when(i == 0)
    def _():
        _gather_rows(y_hbm, cur_idx, n_rows, buf.at[0], sem.at[0])

    @pl.when(i + 1 < n_tiles)
    def _():
        _gather_rows(y_hbm, nxt_idx, n_rows, buf.at[1 - slot], sem.at[1 - slot])

    h = h_ref[...]
    hb = h.astype(BF16)
    hid = _silu(_dot(hb, sg_ref[...])) * _dot(hb, su_ref[...])
    acc = DN_ALPHA * h + _dot(hid.astype(BF16), sd_ref[...])
    _wait_rows(y_hbm, n_rows, buf.at[slot], sem.at[slot])
    routed = buf[slot, 0:tm, :]
    for k in range(1, TOP_K):
        routed = routed + buf[slot, k * tm:(k + 1) * tm, :]
    o_ref[...] = _layer_norm_rows(acc + routed, g_ref[...], b_ref[...])


def _combine_ln(h, y_sorted, dest, sh_gate, sh_up, sh_down, ln_g, ln_b, *, tm=256):
    t, d = h.shape
    de = sh_gate.shape[1]
    n_tiles = t // tm
    idx3 = dest.reshape(n_tiles, tm, TOP_K).transpose(0, 2, 1).reshape(n_tiles, 1, TOP_K * tm)
    full = lambda shape: pl.BlockSpec(shape, lambda i: (0, 0))
    return pl.pallas_call(
        functools.partial(_combine_body, n_tiles, tm),
        out_shape=jax.ShapeDtypeStruct((t, d), F32),
        grid=(n_tiles,),
        in_specs=[pl.BlockSpec((1, 1, TOP_K * tm), lambda i: (i, 0, 0), memory_space=pltpu.SMEM),
                  pl.BlockSpec((1, 1, TOP_K * tm), lambda i: (jnp.minimum(i + 1, n_tiles - 1), 0, 0),
                               memory_space=pltpu.SMEM),
                  pl.BlockSpec(memory_space=pl.ANY),
                  pl.BlockSpec((tm, d), lambda i: (i, 0)),
                  full((d, de)), full((d, de)), full((de, d)), full((1, d)), full((1, d))],
        out_specs=pl.BlockSpec((tm, d), lambda i: (i, 0)),
        scratch_shapes=[pltpu.VMEM((2, TOP_K * tm, d), F32), pltpu.SemaphoreType.DMA((2,))],
        compiler_params=_params("arbitrary"),
        name="moe_combine_ln",
    )(idx3, idx3, y_sorted, h, sh_gate.astype(BF16), sh_up.astype(BF16), sh_down.astype(BF16),
      ln_g.reshape(1, d), ln_b.reshape(1, d))


def _moe_ln(h, router_w, router_b, exp_gate, exp_up, exp_down, sh_gate, sh_up, sh_down, ln_g, ln_b):
    eidx_t, gate_t = _router(h, router_w, router_b)
    slot_tok, slot_gate, block_e, dest, n_blocks = _dispatch_plan(eidx_t, gate_t)
    y_sorted = _expert_blocks(h, slot_tok, slot_gate, block_e, n_blocks, exp_gate, exp_up, exp_down)
    return _combine_ln(h, y_sorted, dest, sh_gate, sh_up, sh_down, ln_g, ln_b)


def _gdn_pre_body(ts, qkv_ref, w_ref, ba_ref, bat_ref, alog_r, dtb_r, alog_c, dtb_c,
                  q_out, k_out, v_out, beta_out, gcc_out, gcr_out, carry):
    i = pl.program_id(1)

    @pl.when(i == 0)
    def _():
        carry[...] = jnp.zeros_like(carry)

    sub8 = lax.broadcasted_iota(I32, (8, 1), 0)
    n_qk = C_KHEADS
    for hb in range(2 * C_KHEADS + C_VHEADS):
        sl = slice(hb * LANES, (hb + 1) * LANES)
        cur = qkv_ref[:, sl]
        tail = carry[:, sl]
        w = w_ref[:, sl]
        acc = cur * w[C_CONV - 1:C_CONV, :]
        for j in range(1, C_CONV):
            rolled = pltpu.roll(cur, j, 0)
            head = jnp.where(sub8 < j, pltpu.roll(tail, j, 0), rolled[0:8])
            shifted = jnp.concatenate([head, rolled[8:]], axis=0)
            acc = acc + shifted * w[C_CONV - 1 - j:C_CONV - j, :]
        carry[:, sl] = cur[ts - 8:ts, :]
        y = _silu(acc)
        if hb < 2 * n_qk:
            y = y * lax.rsqrt(jnp.sum(y * y, axis=1, keepdims=True) + 1e-6)
            if hb < n_qk:
                q_out[hb] = y * (C_DK ** -0.5)
            else:
                k_out[hb - n_qk] = y
        else:
            v_out[hb - 2 * n_qk] = y

    ba = ba_ref[...]
    beta_out[...] = _sigmoid(ba[:, 0:LANES])
    g_col = -jnp.exp(alog_r[...]) * _softplus(ba[:, LANES:2 * LANES] + dtb_r[...])
    rr = lax.broadcasted_iota(I32, (ts, ts), 0)
    cc = lax.broadcasted_iota(I32, (ts, ts), 1)
    same = (rr // C_CHUNK) == (cc // C_CHUNK)
    gcc_out[...] = _dot((same & (rr >= cc)).astype(F32), g_col, HIGHEST)
    g_row = -jnp.exp(alog_c[...]) * _softplus(bat_ref[C_VHEADS:2 * C_VHEADS, :] + dtb_c[...])
    gcr_out[...] = _dot(g_row, (same & (rr <= cc)).astype(F32), HIGHEST)


def _gdn_pre(p_main, ba, bat, conv_w, a_log, dt_bias, bn, s, *, ts=256):
    t = p_main.shape[0]
    nt = s // ts
    qkv_w = conv_w.shape[1]
    pad_r = lambda z: jnp.pad(z.astype(F32), (0, LANES - C_VHEADS)).reshape(1, LANES)
    col = lambda z: z.astype(F32).reshape(C_VHEADS, 1)
    full = lambda shape: pl.BlockSpec(shape, lambda b, i: (0,) * len(shape))
    hm = lambda nh: pl.BlockSpec((nh, ts, LANES), lambda b, i: (0, b * nt + i, 0))
    return pl.pallas_call(
        functools.partial(_gdn_pre_body, ts),
        out_shape=[jax.ShapeDtypeStruct((C_KHEADS, t, LANES), F32),
                   jax.ShapeDtypeStruct((C_KHEADS, t, LANES), F32),
                   jax.ShapeDtypeStruct((C_VHEADS, t, LANES), F32),
                   jax.ShapeDtypeStruct((t, LANES), F32),
                   jax.ShapeDtypeStruct((t, LANES), F32),
                   jax.ShapeDtypeStruct((C_VHEADS, t), F32)],
        grid=(bn, nt),
        in_specs=[pl.BlockSpec((ts, qkv_w), lambda b, i: (b * nt + i, 0)),
                  full((C_CONV, qkv_w)),
                  pl.BlockSpec((ts, 2 * LANES), lambda b, i: (b * nt + i, 0)),
                  pl.BlockSpec((2 * C_VHEADS, ts), lambda b, i: (0, b * nt + i)),
                  full((1, LANES)), full((1, LANES)), full((C_VHEADS, 1)), full((C_VHEADS, 1))],
        out_specs=[hm(C_KHEADS), hm(C_KHEADS), hm(C_VHEADS),
                   pl.BlockSpec((ts, LANES), lambda b, i: (b * nt + i, 0)),
                   pl.BlockSpec((ts, LANES), lambda b, i: (b * nt + i, 0)),
                   pl.BlockSpec((C_VHEADS, ts), lambda b, i: (0, b * nt + i))],
        scratch_shapes=[pltpu.VMEM((8, qkv_w), F32)],
        compiler_params=_params("parallel", "arbitrary"),
        name="gdn_pre",
    )(p_main, conv_w, ba, bat, pad_r(a_log), pad_r(dt_bias), col(a_log), col(dt_bias))


def _gdn_chunk_body(tb, q_ref, k_ref, v_ref, z_ref, beta_ref, gcc_ref, gcr_ref, ng_ref, o_ref, state):
    c = C_CHUNK
    n = 2 * c
    hk = pl.program_id(1)

    @pl.when(pl.program_id(2) == 0)
    def _():
        state[...] = jnp.zeros_like(state)

    row = lax.broadcasted_iota(I32, (n, n), 0)
    col = lax.broadcasted_iota(I32, (n, n), 1)
    same = (row // c) == (col // c)
    strict = same & (row > col)
    incl = same & (row >= col)
    lane = lax.broadcasted_iota(I32, (1, LANES), 1)
    top = lax.broadcasted_iota(I32, (n, 1), 0) < c
    ng = ng_ref[...]

    def pick(x, hv):
        return jnp.sum(jnp.where(lane == hv, x, 0.0), axis=1, keepdims=True)

    def chunk(ci, _):
        rs = pl.ds(pl.multiple_of(ci * c, c), c)
        q, k = q_ref[rs, :], k_ref[rs, :]
        v = jnp.concatenate([v_ref[0, rs, :], v_ref[1, rs, :]], axis=0)
        bt, gc = beta_ref[rs, :], gcc_ref[rs, :]
        beta = jnp.concatenate([pick(bt, 2 * hk), pick(bt, 2 * hk + 1)], axis=0)
        gcs = jnp.concatenate([pick(gc, 2 * hk), pick(gc, 2 * hk + 1)], axis=0)
        gcr = gcr_ref[pl.ds(ci, 1), :]
        k2 = jnp.concatenate([k, k], axis=0)
        q2 = jnp.concatenate([q, q], axis=0)
        dec = jnp.exp(jnp.where(incl, gcs - gcr, -jnp.inf))
        kk = _dot_nt(k2, k2, HIGHEST)
        qk = _dot_nt(q2, k2, HIGHEST)
        a_blk = jnp.where(strict, kk * beta * dec, 0.0)
        attn = jnp.where(incl, qk * dec, 0.0)
        tinv = _tri_inv(a_blk, c)
        eg = jnp.exp(gcs)
        rhs = jnp.concatenate([v * beta, k2 * (beta * eg)], axis=1)
        sol = _dot(tinv, rhs, HIGHEST)
        u, w = sol[:, 0:LANES], sol[:, LANES:2 * LANES]
        qd = q2 * eg
        gl0, gl1 = gcs[c - 1:c, :], gcs[n - 1:n, :]
        kd = k2 * jnp.exp(jnp.where(top, gl0, gl1) - gcs)
        s0, s1 = state[0], state[1]
        ws0 = _dot(jnp.concatenate([w[0:c], qd[0:c]], axis=0), s0, HIGHEST)
        ws1 = _dot(jnp.concatenate([w[c:n], qd[c:n]], axis=0), s1, HIGHEST)
        v_new = u - jnp.concatenate([ws0[0:c], ws1[0:c]], axis=0)
        o = jnp.concatenate([ws0[c:n], ws1[c:n]], axis=0) + _dot(attn, v_new, HIGHEST)
        state[0] = s0 * jnp.exp(gl0) + _dot_tn(kd[0:c], v_new[0:c], HIGHEST)
        state[1] = s1 * jnp.exp(gl1) + _dot_tn(kd[c:n], v_new[c:n], HIGHEST)
        on = o * lax.rsqrt(jnp.mean(o * o, axis=1, keepdims=True) + C_NORM_EPS) * ng
        o_ref[rs, 0:LANES] = on[0:c] * _silu(z_ref[rs, 0:LANES])
        o_ref[rs, LANES:2 * LANES] = on[c:n] * _silu(z_ref[rs, LANES:2 * LANES])
        return 0

    lax.fori_loop(0, tb // c, chunk, 0)


def _gdn_chunk(qh, kh, vh, p_main, beta_c, gc_c, gc_r, norm_g, bn, s, *, tb=512):
    t = qh.shape[1]
    nt = s // tb
    z_off = (2 * C_KHEADS + C_VHEADS) // 2
    gcr = gc_r.reshape(C_KHEADS, 2, t // C_CHUNK, C_CHUNK).transpose(0, 2, 1, 3).reshape(
        C_KHEADS, t // C_CHUNK, 2 * C_CHUNK)
    tile = lambda f: pl.BlockSpec((None, tb, LANES), f)
    return pl.pallas_call(
        functools.partial(_gdn_chunk_body, tb),
        out_shape=jax.ShapeDtypeStruct((t, C_VHEADS * LANES), F32),
        grid=(bn, C_KHEADS, nt),
        in_specs=[tile(lambda b, h, i: (h, b * nt + i, 0)),
                  tile(lambda b, h, i: (h, b * nt + i, 0)),
                  pl.BlockSpec((2, tb, LANES), lambda b, h, i: (h, b * nt + i, 0)),
                  pl.BlockSpec((tb, 2 * LANES), lambda b, h, i: (b * nt + i, z_off + h)),
                  pl.BlockSpec((tb, LANES), lambda b, h, i: (b * nt + i, 0)),
                  pl.BlockSpec((tb, LANES), lambda b, h, i: (b * nt + i, 0)),
                  pl.BlockSpec((None, tb // C_CHUNK, 2 * C_CHUNK), lambda b, h, i: (h, b * nt + i, 0)),
                  pl.BlockSpec((1, LANES), lambda b, h, i: (0, 0))],
        out_specs=pl.BlockSpec((tb, 2 * LANES), lambda b, h, i: (b * nt + i, h)),
        scratch_shapes=[pltpu.VMEM((2, C_DK, LANES), F32)],
        compiler_params=_params("parallel", "parallel", "arbitrary"),
        name="gdn_chunk",
    )(qh, kh, vh, p_main, beta_c, gc_c, gcr, norm_g.reshape(1, LANES))


def _attn_rwkv_layer(x2, bn, s, w_in, a_lambda, a_norm_g, b_mix, b_w0, b_w2, b_a0, b_a2, b_g2, b_k_k, b_k_a, b_r_k,
                     b_lnx_g, b_lnx_b, w_out, ln_g, ln_b, layer_idx):
    a_cols = 3 * A_HEADS * A_VDIM
    pa = _matmul(x2, w_in[:, :a_cols])
    pb = _matmul(x2, w_in[:, a_cols:], tn=256)
    ya = _diff_attention(pa, a_lambda, a_norm_g, bn, s, layer_idx)
    pre = _rwkv_pre(pb, b_mix, b_w0, b_w2, b_a0, b_a2, b_g2, b_k_k, b_k_a, b_r_k, bn, s)
    yb = _rwkv_scan(*pre, b_lnx_g, b_lnx_b, bn, s)
    wa = A_HEADS * A_VDIM
    return _matmul_res_ln([ya, yb], [w_out[:wa], w_out[wa:]], x2, ln_g, ln_b)


def _gdn_layer(x2, bn, s, w_in, conv_w, a_log, dt_bias, norm_g, w_out, ln_g, ln_b):
    qkv_w = conv_w.shape[1]
    main_w = qkv_w + C_VHEADS * LANES
    w_b = w_in[:, main_w:main_w + C_VHEADS]
    w_a = w_in[:, main_w + C_VHEADS:main_w + 2 * C_VHEADS]
    pad = lambda w: jnp.pad(w, ((0, 0), (0, LANES - C_VHEADS)))
    p_main = _matmul(x2, w_in[:, :main_w])
    ba = _matmul(x2, jnp.concatenate([pad(w_b), pad(w_a)], axis=1), tn=2 * LANES)
    bat = _matmul_t(jnp.concatenate([w_b, w_a], axis=1).T, x2)
    qh, kh, vh, beta_c, gc_c, gc_r = _gdn_pre(p_main, ba, bat, conv_w, a_log, dt_bias, bn, s)
    o = _gdn_chunk(qh, kh, vh, p_main, beta_c, gc_c, gc_r, norm_g, bn, s)
    return _matmul_res_ln([o], [w_out], x2, ln_g, ln_b)


def kernel(x, l0_w_in, l0_a_lambda, l0_a_norm_g, l0_b_mix, l0_b_w0, l0_b_w2, l0_b_a0, l0_b_a2, l0_b_g2, l0_b_k_k, l0_b_k_a, l0_b_r_k, l0_b_lnx_g, l0_b_lnx_b, l0_w_out, l0_ln1_g, l0_ln1_b, l0_router_w, l0_router_b, l0_exp_gate, l0_exp_up, l0_exp_down, l0_sh_gate, l0_sh_up, l0_sh_down, l0_ln2_g, l0_ln2_b, l1_w_in, l1_conv_w, l1_a_log, l1_dt_bias, l1_norm_g, l1_w_out, l1_ln1_g, l1_ln1_b, l1_router_w, l1_router_b, l1_exp_gate, l1_exp_up, l1_exp_down, l1_sh_gate, l1_sh_up, l1_sh_down, l1_ln2_g, l1_ln2_b):
    bn, s, d = x.shape
    x2 = x.reshape(bn * s, d)
    x2 = _attn_rwkv_layer(x2, bn, s, l0_w_in, l0_a_lambda, l0_a_norm_g, l0_b_mix, l0_b_w0, l0_b_w2, l0_b_a0,
                          l0_b_a2, l0_b_g2, l0_b_k_k, l0_b_k_a, l0_b_r_k, l0_b_lnx_g, l0_b_lnx_b, l0_w_out,
                          l0_ln1_g, l0_ln1_b, 0)
    x2 = _moe_ln(x2, l0_router_w, l0_router_b, l0_exp_gate, l0_exp_up, l0_exp_down, l0_sh_gate, l0_sh_up,
                 l0_sh_down, l0_ln2_g, l0_ln2_b)
    x2 = _gdn_layer(x2, bn, s, l1_w_in, l1_conv_w, l1_a_log, l1_dt_bias, l1_norm_g, l1_w_out, l1_ln1_g, l1_ln1_b)
    x2 = _moe_ln(x2, l1_router_w, l1_router_b, l1_exp_gate, l1_exp_up, l1_exp_down, l1_sh_gate, l1_sh_up,
                 l1_sh_down, l1_ln2_g, l1_ln2_b)
    return x2.reshape(bn, s, d)
```

```python
import functools
import math

import jax
import jax.numpy as jnp
import numpy as np
from jax import lax
from jax.experimental import pallas as pl
from jax.experimental.pallas import tpu as pltpu

F32 = jnp.float32
BF16 = jnp.bfloat16
I32 = jnp.int32
HIGHEST = lax.Precision.HIGHEST

LANES = 128
TILE_ROWS = 8
VMEM_LIMIT = 56 * 1024 * 1024

DEPTH = 2
DN_ALPHA = (2.0 * DEPTH) ** 0.25
LN_EPS = 1e-5

A_HEADS = 4
A_VDIM = 128
A_QKDIM = 64
A_NORM_EPS = 1e-5

B_HEAD = 64
B_WIDTH = 512
B_LNX_EPS = 64e-5
B_CHUNK = 64

C_DK = 128
C_KHEADS = 8
C_VHEADS = 16
C_CONV = 4
C_CHUNK = 64
C_NORM_EPS = 1e-6

N_EXPERTS = 64
TOP_K = 8
N_GROUPS = 8
TOPK_GROUPS = 4
ROUTE_SCALE = 2.5
MOE_ROWS = 256


def _params(*sem):
    return pltpu.CompilerParams(dimension_semantics=sem, vmem_limit_bytes=VMEM_LIMIT)


def _dot(a, b, precision=None):
    return jnp.dot(a, b, preferred_element_type=F32, precision=precision)


def _dot_nt(a, b, precision=None):
    return lax.dot_general(a, b, (((1,), (1,)), ((), ())), preferred_element_type=F32, precision=precision)


def _dot_tn(a, b, precision=None):
    return lax.dot_general(a, b, (((0,), (0,)), ((), ())), preferred_element_type=F32, precision=precision)


def _split(x):
    hi = x.astype(BF16)
    return hi, (x - hi.astype(F32)).astype(BF16)


def _dot_sel(sel, x):
    hi, lo = _split(x)
    return _dot(sel, hi) + _dot(sel, lo)


def _dot_by_sel(x, sel):
    hi, lo = _split(x)
    return _dot(hi, sel) + _dot(lo, sel)


def _dot3(a, b):
    ah, al = _split(a)
    bh, bl = _split(b)
    return _dot(ah, bh) + (_dot(ah, bl) + _dot(al, bh))


def _bdot(a, b):
    return _dot(a.astype(BF16), b.astype(BF16))


def _bdot_nt(a, b):
    return _dot_nt(a.astype(BF16), b.astype(BF16))


def _bdot_tn(a, b):
    return _dot_tn(a.astype(BF16), b.astype(BF16))


def _sigmoid(x):
    return 1.0 / (1.0 + jnp.exp(-x))


def _silu(x):
    return x * _sigmoid(x)


def _softplus(x):
    return jnp.maximum(x, 0.0) + jnp.log(1.0 + jnp.exp(-jnp.abs(x)))


def _mm_body(x_ref, w_ref, o_ref):
    o_ref[...] = _dot(x_ref[...].astype(BF16), w_ref[...]).astype(o_ref.dtype)


def _matmul(x, w, *, tm=512, tn=512, out_dtype=F32):
    m, k = x.shape
    n = w.shape[1]
    tn = min(tn, n)
    assert m % tm == 0 and n % tn == 0
    return pl.pallas_call(
        _mm_body,
        out_shape=jax.ShapeDtypeStruct((m, n), out_dtype),
        grid=(m // tm, n // tn),
        in_specs=[pl.BlockSpec((tm, k), lambda i, j: (i, 0)),
                  pl.BlockSpec((k, tn), lambda i, j: (0, j))],
        out_specs=pl.BlockSpec((tm, tn), lambda i, j: (i, j)),
        compiler_params=_params("parallel", "arbitrary"),
        name="matmul",
    )(x, w.astype(BF16))


def _mm_nt_body(w_ref, x_ref, o_ref):
    o_ref[...] = _dot_nt(w_ref[...], x_ref[...].astype(BF16))


def _matmul_t(wt, x, *, tm=512):
    n, k = wt.shape
    m = x.shape[0]
    return pl.pallas_call(
        _mm_nt_body,
        out_shape=jax.ShapeDtypeStruct((n, m), F32),
        grid=(m // tm,),
        in_specs=[pl.BlockSpec((n, k), lambda i: (0, 0)),
                  pl.BlockSpec((tm, k), lambda i: (i, 0))],
        out_specs=pl.BlockSpec((n, tm), lambda i: (0, i)),
        compiler_params=_params("parallel"),
        name="matmul_t",
    )(wt.astype(BF16), x)


def _layer_norm_rows(y, g, b):
    mu = jnp.mean(y, axis=-1, keepdims=True)
    d = y - mu
    var = jnp.mean(d * d, axis=-1, keepdims=True)
    return d * lax.rsqrt(var + LN_EPS) * g + b


def _mm_ln_body(n_parts, tm, *refs):
    parts = refs[:n_parts]
    ws = refs[n_parts:2 * n_parts]
    res_ref, g_ref, b_ref, o_ref, o8_ref = refs[2 * n_parts:]
    acc = DN_ALPHA * res_ref[...]
    for p_ref, w_ref in zip(parts, ws):
        acc = acc + _dot(p_ref[...].astype(BF16), w_ref[...])
    y = _layer_norm_rows(acc, g_ref[...], b_ref[...])
    o_ref[...] = y
    _to_tile_rows(o8_ref, y, tm)


def _matmul_res_ln(parts, ws, res, g, b, *, tm=256):
    m, d = res.shape
    assert d == TILE_ROWS * LANES
    n_parts = len(parts)
    in_specs = [pl.BlockSpec((tm, p.shape[1]), lambda i: (i, 0)) for p in parts]
    in_specs += [pl.BlockSpec(w.shape, lambda i: (0, 0)) for w in ws]
    in_specs += [pl.BlockSpec((tm, d), lambda i: (i, 0)),
                 pl.BlockSpec((1, d), lambda i: (0, 0)),
                 pl.BlockSpec((1, d), lambda i: (0, 0))]
    return pl.pallas_call(
        functools.partial(_mm_ln_body, n_parts, tm),
        out_shape=[jax.ShapeDtypeStruct((m, d), F32), jax.ShapeDtypeStruct((m * TILE_ROWS, LANES), F32)],
        grid=(m // tm,),
        in_specs=in_specs,
        out_specs=[pl.BlockSpec((tm, d), lambda i: (i, 0)), pl.BlockSpec((tm * TILE_ROWS, LANES), lambda i: (i, 0))],
        compiler_params=_params("parallel"),
        name="matmul_res_ln",
    )(*parts, *[w.astype(BF16) for w in ws], res, g.reshape(1, d), b.reshape(1, d))


def _attn_body(seq_blocks, tq, lam_init, slopes_ref, lam_ref, g_ref, q_ref, k_ref, v_ref, o_ref):
    h = pl.program_id(1)
    qi = pl.program_id(2)
    slope = slopes_ref[h]
    lp = lam_ref[...]
    lam = (jnp.exp(jnp.sum(lp[0:1] * lp[1:2], axis=1, keepdims=True))
           - jnp.exp(jnp.sum(lp[2:3] * lp[3:4], axis=1, keepdims=True)) + lam_init)
    lane = lax.broadcasted_iota(I32, (1, A_VDIM), 1)
    q = q_ref[...] * (A_QKDIM ** -0.5)
    q0 = jnp.where(lane < A_QKDIM, q, 0.0).astype(BF16)
    q1 = jnp.where(lane >= A_QKDIM, q, 0.0).astype(BF16)
    rel = (lax.broadcasted_iota(I32, (tq, tq), 0) - lax.broadcasted_iota(I32, (tq, tq), 1))

    def step(j, carry):
        m0, l0, a0, m1, l1, a1 = carry
        kb = k_ref[pl.ds(pl.multiple_of(j * tq, tq), tq), :].astype(BF16)
        vb = v_ref[pl.ds(pl.multiple_of(j * tq, tq), tq), :].astype(BF16)
        dist = rel + (qi - j) * tq
        bias = slope * dist.astype(F32)
        keep = dist >= 0

        def one(qc, m, l, a):
            s = jnp.where(keep, _dot_nt(qc, kb) - bias, -jnp.inf)
            m_new = jnp.maximum(m, jnp.max(s, axis=1, keepdims=True))
            alpha = jnp.exp(m - m_new)
            p = jnp.exp(s - m_new)
            l_new = alpha * l + jnp.sum(p, axis=1, keepdims=True)
            a_new = alpha * a + _dot(p.astype(BF16), vb)
            return m_new, l_new, a_new

        m0, l0, a0 = one(q0, m0, l0, a0)
        m1, l1, a1 = one(q1, m1, l1, a1)
        return m0, l0, a0, m1, l1, a1

    neg = jnp.full((tq, 1), -jnp.inf, F32)
    zero1 = jnp.zeros((tq, 1), F32)
    zero = jnp.zeros((tq, A_VDIM), F32)
    _, l0, a0, _, l1, a1 = lax.fori_loop(0, qi + 1, step, (neg, zero1, zero, neg, zero1, zero))
    o = a0 / l0 - lam * (a1 / l1)
    ms = jnp.mean(o * o, axis=1, keepdims=True)
    o_ref[...] = o * lax.rsqrt(ms + A_NORM_EPS) * g_ref[...] * (1.0 - lam_init)


def _diff_attention(pa, lam_params, norm_g, bn, s, layer_idx, *, tq=256):
    t = pa.shape[0]
    nq = s // tq
    lam_init = 0.8 - 0.6 * math.exp(-0.3 * layer_idx)
    slopes = jnp.asarray(np.array([2.0 ** (-8.0 * (i + 1) / A_HEADS) for i in range(A_HEADS)], dtype=np.float32))
    body = functools.partial(_attn_body, nq, tq, lam_init)
    return pl.pallas_call(
        body,
        out_shape=jax.ShapeDtypeStruct((t, A_HEADS * A_VDIM), F32),
        grid=(bn, A_HEADS, nq),
        in_specs=[pl.BlockSpec(memory_space=pltpu.SMEM),
                  pl.BlockSpec((4, A_QKDIM), lambda b, h, i: (0, 0)),
                  pl.BlockSpec((1, A_VDIM), lambda b, h, i: (0, 0)),
                  pl.BlockSpec((tq, A_VDIM), lambda b, h, i: (b * nq + i, h)),
                  pl.BlockSpec((s, A_VDIM), lambda b, h, i: (b, A_HEADS + h)),
                  pl.BlockSpec((s, A_VDIM), lambda b, h, i: (b, 2 * A_HEADS + h))],
        out_specs=pl.BlockSpec((tq, A_VDIM), lambda b, h, i: (b * nq + i, h)),
        compiler_params=_params("parallel", "parallel", "arbitrary"),
        name="diff_attention",
    )(slopes, lam_params.astype(F32), norm_g.reshape(1, A_VDIM), pa, pa, pa)


def _tri_inv(a, blk):
    n = a.shape[0]
    row = lax.broadcasted_iota(I32, (n, n), 0)
    col = lax.broadcasted_iota(I32, (n, n), 1)
    eye = (row == col).astype(F32)

    def same(m):
        return (row // m) == (col // m)

    d = jnp.where(same(8), a, 0.0)
    d2 = _dot3(d, d)
    d4 = _dot3(d2, d2)
    t = _dot3(_dot3(eye - d, eye + d2), eye + d4)
    m = 8
    while m < blk:
        e = jnp.where(same(2 * m) & jnp.logical_not(same(m)), a, 0.0)
        t = t - _dot3(t, _dot3(e, t))
        m *= 2
    return t


def _group_sum(x, ones_blk):
    return _dot_by_sel(x, ones_blk)


def _rwkv_pre_body(ts, pb_ref, mix_ref, w0_ref, w2_ref, a0_ref, a2_ref, g2_ref, kk_ref, ka_ref, rk_ref, ones_ref,
                   r_out, k_out, v_out, kap_out, b_out, ld_out, g_out, bv_out, carry):
    i = pl.program_id(1)

    @pl.when(i == 0)
    def _():
        carry[...] = jnp.zeros_like(carry)

    cur = pb_ref[...]
    rolled = pltpu.roll(cur, 1, 0)
    first = lax.broadcasted_iota(I32, (ts, 1), 0) == 0
    prev = jnp.where(first, carry[0:1, :], rolled)
    carry[0:1, :] = cur[ts - 1:ts, :]
    x = cur + (prev - cur) * mix_ref[...]
    r = x[:, 0:B_WIDTH]
    k = x[:, B_WIDTH:2 * B_WIDTH]
    v = x[:, 2 * B_WIDTH:3 * B_WIDTH]
    xwa = x[:, 3 * B_WIDTH:3 * B_WIDTH + LANES]
    xg = x[:, 3 * B_WIDTH + LANES:3 * B_WIDTH + 2 * LANES]
    w_log = -_softplus(-(w0_ref[...] + _dot(jnp.tanh(xwa).astype(BF16), w2_ref[...]))) - 0.5
    a = _sigmoid(a0_ref[...] + _dot(xwa.astype(BF16), a2_ref[...]))
    g = _dot(_sigmoid(xg).astype(BF16), g2_ref[...])
    ones_blk = ones_ref[...]
    kkr = k * kk_ref[...]
    kp = k * (1.0 + (a - 1.0) * ka_ref[...])
    rkr = r * kp * rk_ref[...]
    for c in range(B_WIDTH // LANES):
        sl = slice(c * LANES, (c + 1) * LANES)
        kn = kkr[:, sl]
        kap = kn * lax.rsqrt(_group_sum(kn * kn, ones_blk) + 1e-6)
        kap_out[:, sl] = kap
        b_out[:, sl] = kap * a[:, sl]
        bv_out[:, sl] = _group_sum(rkr[:, sl], ones_blk) * v[:, sl]
    r_out[...] = r
    k_out[...] = kp
    v_out[...] = v
    ld_out[...] = -jnp.exp(w_log)
    g_out[...] = g


def _head_group_ones():
    idx = np.arange(LANES) // B_HEAD
    return jnp.asarray((idx[:, None] == idx[None, :]).astype(np.float32)).astype(BF16)


def _rwkv_pre(pb, mix, w0, w2, a0, a2, g2, k_k, k_a, r_k, bn, s, *, ts=256):
    t, cols = pb.shape
    nt = s // ts
    lora = w2.shape[0]
    w2p = jnp.concatenate([w2, jnp.zeros_like(w2)], axis=0).astype(BF16)
    a2p = jnp.concatenate([jnp.zeros_like(a2), a2], axis=0).astype(BF16)
    assert lora * 2 == LANES
    row = lambda z: z.reshape(1, -1)
    full = lambda shape: pl.BlockSpec(shape, lambda b, i: (0, 0))
    out = jax.ShapeDtypeStruct((t, B_WIDTH), F32)
    tile = pl.BlockSpec((ts, B_WIDTH), lambda b, i: (b * nt + i, 0))
    return pl.pallas_call(
        functools.partial(_rwkv_pre_body, ts),
        out_shape=[out] * 8,
        grid=(bn, nt),
        in_specs=[pl.BlockSpec((ts, cols), lambda b, i: (b * nt + i, 0)),
                  full((1, cols)), full((1, B_WIDTH)), full((LANES, B_WIDTH)), full((1, B_WIDTH)),
                  full((LANES, B_WIDTH)), full((LANES, B_WIDTH)), full((1, B_WIDTH)), full((1, B_WIDTH)),
                  full((1, B_WIDTH)), full((LANES, LANES))],
        out_specs=[tile] * 8,
        scratch_shapes=[pltpu.VMEM((8, cols), F32)],
        compiler_params=_params("parallel", "arbitrary"),
        name="rwkv_pre",
    )(pb, row(mix), row(w0), w2p, row(a0), a2p, g2.astype(BF16), row(k_k), row(k_a), row(r_k), _head_group_ones())


def _rwkv_scan_body(c, r_ref, k_ref, v_ref, kap_ref, b_ref, ld_ref, g_ref, bv_ref, lng_ref, lnb_ref, ones_ref,
                    o_ref, state):
    @pl.when(pl.program_id(1) == 0)
    def _():
        state[...] = jnp.zeros_like(state)

    n = 2 * c
    row = lax.broadcasted_iota(I32, (n, n), 0)
    col = lax.broadcasted_iota(I32, (n, n), 1)
    same = (row // c) == (col // c)
    strict = same & (row > col)
    incl = same & (row >= col)
    trow = lax.broadcasted_iota(I32, (c, c), 0)
    tcol = lax.broadcasted_iota(I32, (c, c), 1)
    tril = (trow >= tcol).astype(BF16)
    head0 = lax.broadcasted_iota(I32, (1, LANES), 1) < B_HEAD
    ones_blk = ones_ref[...]

    def sel(z):
        return jnp.where(head0, z[0:c], z[c:n])

    def dup(z):
        return jnp.concatenate([z, z], axis=0)

    for p in range(B_WIDTH // LANES):
        sl = slice(p * LANES, (p + 1) * LANES)
        ld = ld_ref[:, sl]
        r, k, v, kap, b = r_ref[:, sl], k_ref[:, sl], v_ref[:, sl], kap_ref[:, sl], b_ref[:, sl]
        cum = _dot_sel(tril, ld)
        gam = jnp.exp(cum)
        einv = jnp.exp(-cum)
        kt = kap * jnp.exp(cum - ld)
        rt = r * gam
        kh = k * einv
        bh = b * einv
        g_last = gam[c - 1:c, :]
        x = jnp.concatenate([jnp.where(head0, kt, 0.0), jnp.where(head0, 0.0, kt),
                             jnp.where(head0, rt, 0.0), jnp.where(head0, 0.0, rt)], axis=0)
        y = jnp.concatenate([kh, kh, bh, bh], axis=0)
        m2 = _bdot_nt(x, y)
        a_kk = jnp.where(strict, m2[0:n, 0:n], 0.0)
        a_bk = jnp.where(strict, m2[0:n, n:2 * n], 0.0)
        b_rk = jnp.where(incl, m2[n:2 * n, 0:n], 0.0)
        b_rb = jnp.where(incl, m2[n:2 * n, n:2 * n], 0.0)
        tinv = _tri_inv(a_bk, c)
        s_old = state[p]
        ps = _bdot_nt(jnp.concatenate([kt, rt], axis=0), s_old)
        vv = dup(v)
        rhs = ps[0:c] + sel(_bdot(a_kk, vv))
        u = sel(_bdot(tinv, dup(rhs)))
        yc = ps[c:n] + sel(_bdot(b_rk, vv) - _bdot(b_rb, dup(u)))
        upd = _bdot_tn(jnp.concatenate([v, u], axis=0),
                       jnp.concatenate([kh * g_last, -(bh * g_last)], axis=0))
        state[p] = s_old * g_last + jnp.where(same, upd, 0.0)
        mu = _group_sum(yc, ones_blk) * (1.0 / B_HEAD)
        dy = yc - mu
        var = _group_sum(dy * dy, ones_blk) * (1.0 / B_HEAD)
        yn = dy * lax.rsqrt(var + B_LNX_EPS) * lng_ref[:, sl] + lnb_ref[:, sl]
        o_ref[:, sl] = (yn + bv_ref[:, sl]) * g_ref[:, sl]


def _rwkv_scan(r, k, v, kap, b, ld, g, bv, lnx_g, lnx_b, bn, s, *, c=B_CHUNK):
    t = r.shape[0]
    nc = s // c
    tile = pl.BlockSpec((c, B_WIDTH), lambda bi, i: (bi * nc + i, 0))
    vec = pl.BlockSpec((1, B_WIDTH), lambda bi, i: (0, 0))
    return pl.pallas_call(
        functools.partial(_rwkv_scan_body, c),
        out_shape=jax.ShapeDtypeStruct((t, B_WIDTH), F32),
        grid=(bn, nc),
        in_specs=[tile] * 8 + [vec, vec, pl.BlockSpec((LANES, LANES), lambda bi, i: (0, 0))],
        out_specs=tile,
        scratch_shapes=[pltpu.VMEM((B_WIDTH // LANES, LANES, LANES), F32)],
        compiler_params=_params("parallel", "arbitrary"),
        name="rwkv_scan",
    )(r, k, v, kap, b, ld, g, bv, lnx_g.reshape(1, -1), lnx_b.reshape(1, -1), _head_group_ones())


def _first_index_of_max(vals, idx, sentinel):
    m = jnp.max(vals, axis=0, keepdims=True)
    first = jnp.min(jnp.where(vals == m, idx, sentinel), axis=0, keepdims=True)
    return m, first


def _router_body(tm, w_ref, bias_ref, h_ref, eidx_ref, gate_ref, rank_ref, count_ref, carry):
    @pl.when(pl.program_id(0) == 0)
    def _():
        carry[...] = jnp.zeros_like(carry)

    per_group = N_EXPERTS // N_GROUPS
    logits = _dot_nt(w_ref[...], h_ref[...], HIGHEST)
    scores = _sigmoid(logits)
    choice = scores + bias_ref[...]
    sub = lax.broadcasted_iota(I32, (per_group, tm), 0)
    grp_rows = []
    for g in range(N_GROUPS):
        cg = choice[g * per_group:(g + 1) * per_group, :]
        m1, first = _first_index_of_max(cg, sub, per_group)
        m2 = jnp.max(jnp.where(sub == first, -jnp.inf, cg), axis=0, keepdims=True)
        grp_rows.append(m1 + m2)
    gs = jnp.concatenate(grp_rows, axis=0)
    gidx = lax.broadcasted_iota(I32, (N_GROUPS, tm), 0)
    gsel = jnp.zeros((N_GROUPS, tm), jnp.bool_)
    for _ in range(TOPK_GROUPS):
        _, first = _first_index_of_max(gs, gidx, N_GROUPS)
        pick = gidx == first
        gsel = gsel | pick
        gs = jnp.where(pick, -jnp.inf, gs)
    ch = jnp.concatenate(
        [jnp.where(gsel[g:g + 1, :], choice[g * per_group:(g + 1) * per_group, :], -jnp.inf)
         for g in range(N_GROUPS)], axis=0)
    eidx = lax.broadcasted_iota(I32, (N_EXPERTS, tm), 0)
    idx_rows, score_rows, picks = [], [], []
    for _ in range(TOP_K):
        _, first = _first_index_of_max(ch, eidx, N_EXPERTS)
        pick = eidx == first
        picks.append(pick)
        idx_rows.append(first)
        score_rows.append(jnp.sum(jnp.where(pick, scores, 0.0), axis=0, keepdims=True))
        ch = jnp.where(pick, -jnp.inf, ch)
    sc = jnp.concatenate(score_rows, axis=0)
    eidx_ref[...] = jnp.concatenate(idx_rows, axis=0)
    gate_ref[...] = sc / jnp.sum(sc, axis=0, keepdims=True) * ROUTE_SCALE

    onehot = picks[0].astype(F32)
    for pick in picks[1:]:
        onehot = onehot + pick.astype(F32)
    before = (lax.broadcasted_iota(I32, (tm, tm), 0) < lax.broadcasted_iota(I32, (tm, tm), 1)).astype(BF16)
    seen = carry[...] + _dot(onehot.astype(BF16), before)
    rank_ref[...] = jnp.concatenate(
        [jnp.sum(jnp.where(pick, seen, 0.0), axis=0, keepdims=True) for pick in picks], axis=0).astype(I32)
    total = carry[...] + jnp.sum(onehot, axis=1, keepdims=True)
    carry[...] = total
    count_ref[...] = total.astype(I32)


def _router(h, router_w, router_b, *, tm=512):
    t, d = h.shape
    row_blk = pl.BlockSpec((TOP_K, tm), lambda i: (0, i))
    return pl.pallas_call(
        functools.partial(_router_body, tm),
        out_shape=[jax.ShapeDtypeStruct((TOP_K, t), I32), jax.ShapeDtypeStruct((TOP_K, t), F32),
                   jax.ShapeDtypeStruct((TOP_K, t), I32), jax.ShapeDtypeStruct((N_EXPERTS, 1), I32)],
        grid=(t // tm,),
        in_specs=[pl.BlockSpec((N_EXPERTS, d), lambda i: (0, 0)),
                  pl.BlockSpec((N_EXPERTS, 1), lambda i: (0, 0)),
                  pl.BlockSpec((tm, d), lambda i: (i, 0))],
        out_specs=[row_blk, row_blk, row_blk, pl.BlockSpec((N_EXPERTS, 1), lambda i: (0, 0))],
        scratch_shapes=[pltpu.VMEM((N_EXPERTS, 1), F32)],
        compiler_params=_params("arbitrary"),
        name="moe_router",
    )(router_w.T, router_b.reshape(N_EXPERTS, 1), h)


def _block_plan(counts, n_assign):
    counts = counts.reshape(N_EXPERTS)
    padded = (counts + MOE_ROWS - 1) // MOE_ROWS * MOE_ROWS
    pend = jnp.cumsum(padded)
    pstart = (pend - padded).astype(I32)
    n_blocks = (n_assign + N_EXPERTS * (MOE_ROWS - 1) + MOE_ROWS - 1) // MOE_ROWS
    starts = jnp.arange(n_blocks, dtype=I32) * MOE_ROWS
    block_e = jnp.minimum(jnp.searchsorted(pend, starts, side='right'), N_EXPERTS - 1).astype(I32)
    live = jnp.clip(counts[block_e] - (starts - pstart[block_e]), 0, MOE_ROWS).astype(I32)
    return pstart, block_e, live, n_blocks


def _tile_major(idx_t, tm):
    k, t = idx_t.shape
    return idx_t.reshape(k, t // tm, tm).transpose(1, 0, 2).reshape(t // tm, 1, k * tm)


def _rows_of(tok):
    return pl.ds(pl.multiple_of(tok * TILE_ROWS, TILE_ROWS), TILE_ROWS)


def _to_tile_rows(ref, y, n):
    for j in range(TILE_ROWS):
        ref[pl.ds(j, n, stride=TILE_ROWS), :] = y[:, j * LANES:(j + 1) * LANES]


def _dispatch_body(tm, n_blocks, pstart_ref, live_ref, eidx_ref, rank_ref, h8_ref, xs_hbm, zeros, sem, zsem):
    @pl.when(pl.program_id(0) == 0)
    def _():
        zeros[...] = jnp.zeros_like(zeros)
        rows = MOE_ROWS * TILE_ROWS

        def fill(wait):
            def body(b, _):
                @pl.when(live_ref[b] < MOE_ROWS)
                def _():
                    cp = pltpu.make_async_copy(zeros, xs_hbm.at[pl.ds(pl.multiple_of(b * rows, rows), rows), :],
                                               zsem)
                    cp.wait() if wait else cp.start()
                return 0
            return body

        lax.fori_loop(0, n_blocks, fill(False), 0)
        lax.fori_loop(0, n_blocks, fill(True), 0)

    def issue(tok, _):
        src = h8_ref.at[_rows_of(tok), :]
        for k in range(TOP_K):
            slot = pstart_ref[eidx_ref[0, 0, k * tm + tok]] + rank_ref[0, 0, k * tm + tok]
            pltpu.make_async_copy(src, xs_hbm.at[_rows_of(slot), :], sem).start()
        return 0

    lax.fori_loop(0, tm, issue, 0)
    for _ in range(TOP_K):
        pltpu.make_async_copy(h8_ref, xs_hbm.at[pl.ds(0, tm * TILE_ROWS), :], sem).wait()


def _dispatch(h8, pstart, live, eidx_t, rank_t, n_blocks, *, tm=256):
    t = eidx_t.shape[1]
    grid_spec = pltpu.PrefetchScalarGridSpec(
        num_scalar_prefetch=2,
        grid=(t // tm,),
        in_specs=[pl.BlockSpec((1, 1, TOP_K * tm), lambda i, ps, lv: (i, 0, 0), memory_space=pltpu.SMEM),
                  pl.BlockSpec((1, 1, TOP_K * tm), lambda i, ps, lv: (i, 0, 0), memory_space=pltpu.SMEM),
                  pl.BlockSpec((tm * TILE_ROWS, LANES), lambda i, ps, lv: (i, 0))],
        out_specs=pl.BlockSpec(memory_space=pl.ANY),
        scratch_shapes=[pltpu.VMEM((MOE_ROWS * TILE_ROWS, LANES), F32),
                        pltpu.SemaphoreType.DMA(()), pltpu.SemaphoreType.DMA(())],
    )
    return pl.pallas_call(
        functools.partial(_dispatch_body, tm, n_blocks),
        out_shape=jax.ShapeDtypeStruct((n_blocks * MOE_ROWS * TILE_ROWS, LANES), F32),
        grid_spec=grid_spec,
        compiler_params=_params("arbitrary"),
        name="moe_dispatch",
    )(pstart, live, _tile_major(eidx_t, tm), _tile_major(rank_t, tm), h8)


def _expert_body(be_ref, live_ref, x_ref, wg_ref, wu_ref, wd_ref, o_ref):
    n_live = live_ref[pl.program_id(0)]

    @pl.when(n_live > 0)
    def _():
        x = jnp.concatenate([x_ref[pl.ds(j, MOE_ROWS, stride=TILE_ROWS), :] for j in range(TILE_ROWS)], axis=1)
        x = x.astype(BF16)
        hid = _silu(_dot(x, wg_ref[...])) * _dot(x, wu_ref[...])
        _to_tile_rows(o_ref, _dot(hid.astype(BF16), wd_ref[...]), MOE_ROWS)

    @pl.when(n_live == 0)
    def _():
        o_ref[...] = jnp.zeros_like(o_ref)


def _expert_blocks(xs, block_e, live, n_blocks, wg, wu, wd):
    d, de = wg.shape[1], wg.shape[2]
    rows = MOE_ROWS * TILE_ROWS
    grid_spec = pltpu.PrefetchScalarGridSpec(
        num_scalar_prefetch=2,
        grid=(n_blocks,),
        in_specs=[pl.BlockSpec((rows, LANES), lambda i, be, lv: (i, 0)),
                  pl.BlockSpec((None, d, de), lambda i, be, lv: (be[i], 0, 0)),
                  pl.BlockSpec((None, d, de), lambda i, be, lv: (be[i], 0, 0)),
                  pl.BlockSpec((None, de, d), lambda i, be, lv: (be[i], 0, 0))],
        out_specs=pl.BlockSpec((rows, LANES), lambda i, be, lv: (i, 0)),
    )
    return pl.pallas_call(
        _expert_body,
        out_shape=jax.ShapeDtypeStruct((n_blocks * rows, LANES), F32),
        grid_spec=grid_spec,
        compiler_params=_params("arbitrary"),
        name="moe_experts",
    )(block_e, live, xs, wg.astype(BF16), wu.astype(BF16), wd.astype(BF16))


def _combine_body(n_tiles, tm, pstart_ref, cur_e, cur_r, nxt_e, nxt_r, ys_hbm, h_ref, gate_ref,
                  sg_ref, su_ref, sd_ref, g_ref, b_ref, o_ref, buf, sem):
    i = pl.program_id(0)
    slot = i % 2

    def gather(e_ref, r_ref, dst, dsem):
        def issue(tok, _):
            for k in range(TOP_K):
                src = pstart_ref[e_ref[0, 0, k * tm + tok]] + r_ref[0, 0, k * tm + tok]
                pltpu.make_async_copy(ys_hbm.at[_rows_of(src), :], dst.at[_rows_of(k * tm + tok), :], dsem).start()
            return 0
        lax.fori_loop(0, tm, issue, 0)

    @pl.when(i == 0)
    def _():
        gather(cur_e, cur_r, buf.at[0], sem.at[0])

    @pl.when(i + 1 < n_tiles)
    def _():
        gather(nxt_e, nxt_r, buf.at[1 - slot], sem.at[1 - slot])

    h = h_ref[...]
    hb = h.astype(BF16)
    hid = _silu(_dot(hb, sg_ref[...])) * _dot(hb, su_ref[...])
    acc = DN_ALPHA * h + _dot(hid.astype(BF16), sd_ref[...])
    pltpu.make_async_copy(ys_hbm.at[pl.ds(0, TOP_K * tm * TILE_ROWS), :], buf.at[slot], sem.at[slot]).wait()
    gates = gate_ref[...]
    cols = []
    for j in range(TILE_ROWS):
        part = None
        for k in range(TOP_K):
            rows = buf[slot, pl.ds(k * tm * TILE_ROWS + j, tm, stride=TILE_ROWS), :]
            term = gates[:, k:k + 1] * rows
            part = term if part is None else part + term
        cols.append(part)
    o_ref[...] = _layer_norm_rows(acc + jnp.concatenate(cols, axis=1), g_ref[...], b_ref[...])


def _combine_ln(h, ys, pstart, eidx_t, rank_t, gate_t, sh_gate, sh_up, sh_down, ln_g, ln_b, *, tm=256):
    t, d = h.shape
    de = sh_gate.shape[1]
    n_tiles = t // tm
    e3, r3 = _tile_major(eidx_t, tm), _tile_major(rank_t, tm)
    cur = pl.BlockSpec((1, 1, TOP_K * tm), lambda i, ps: (i, 0, 0), memory_space=pltpu.SMEM)
    nxt = pl.BlockSpec((1, 1, TOP_K * tm), lambda i, ps: (jnp.minimum(i + 1, n_tiles - 1), 0, 0),
                       memory_space=pltpu.SMEM)
    full = lambda shape: pl.BlockSpec(shape, lambda i, ps: (0, 0))
    grid_spec = pltpu.PrefetchScalarGridSpec(
        num_scalar_prefetch=1,
        grid=(n_tiles,),
        in_specs=[cur, cur, nxt, nxt,
                  pl.BlockSpec(memory_space=pl.ANY),
                  pl.BlockSpec((tm, d), lambda i, ps: (i, 0)),
                  pl.BlockSpec((tm, TOP_K), lambda i, ps: (i, 0)),
                  full((d, de)), full((d, de)), full((de, d)), full((1, d)), full((1, d))],
        out_specs=pl.BlockSpec((tm, d), lambda i, ps: (i, 0)),
        scratch_shapes=[pltpu.VMEM((2, TOP_K * tm * TILE_ROWS, LANES), F32), pltpu.SemaphoreType.DMA((2,))],
    )
    return pl.pallas_call(
        functools.partial(_combine_body, n_tiles, tm),
        out_shape=jax.ShapeDtypeStruct((t, d), F32),
        grid_spec=grid_spec,
        compiler_params=_params("arbitrary"),
        name="moe_combine_ln",
    )(pstart, e3, r3, e3, r3, ys, h, gate_t.T, sh_gate.astype(BF16), sh_up.astype(BF16), sh_down.astype(BF16),
      ln_g.reshape(1, d), ln_b.reshape(1, d))


def _moe_ln(h, h8, router_w, router_b, exp_gate, exp_up, exp_down, sh_gate, sh_up, sh_down, ln_g, ln_b):
    eidx_t, gate_t, rank_t, counts = _router(h, router_w, router_b)
    pstart, block_e, live, n_blocks = _block_plan(counts, eidx_t.size)
    xs = _dispatch(h8, pstart, live, eidx_t, rank_t, n_blocks)
    ys = _expert_blocks(xs, block_e, live, n_blocks, exp_gate, exp_up, exp_down)
    return _combine_ln(h, ys, pstart, eidx_t, rank_t, gate_t, sh_gate, sh_up, sh_down, ln_g, ln_b)


def _gdn_pre_body(ts, qkv_ref, w_ref, ba_ref, bat_ref, alog_r, dtb_r, alog_c, dtb_c,
                  q_out, k_out, v_out, beta_out, gcc_out, gcr_out, carry):
    i = pl.program_id(1)

    @pl.when(i == 0)
    def _():
        carry[...] = jnp.zeros_like(carry)

    sub8 = lax.broadcasted_iota(I32, (8, 1), 0)
    n_qk = C_KHEADS
    for hb in range(2 * C_KHEADS + C_VHEADS):
        sl = slice(hb * LANES, (hb + 1) * LANES)
        cur = qkv_ref[:, sl]
        tail = carry[:, sl]
        w = w_ref[:, sl]
        acc = cur * w[C_CONV - 1:C_CONV, :]
        for j in range(1, C_CONV):
            rolled = pltpu.roll(cur, j, 0)
            head = jnp.where(sub8 < j, pltpu.roll(tail, j, 0), rolled[0:8])
            shifted = jnp.concatenate([head, rolled[8:]], axis=0)
            acc = acc + shifted * w[C_CONV - 1 - j:C_CONV - j, :]
        carry[:, sl] = cur[ts - 8:ts, :]
        y = _silu(acc)
        if hb < 2 * n_qk:
            y = y * lax.rsqrt(jnp.sum(y * y, axis=1, keepdims=True) + 1e-6)
            if hb < n_qk:
                q_out[hb] = y * (C_DK ** -0.5)
            else:
                k_out[hb - n_qk] = y
        else:
            v_out[hb - 2 * n_qk] = y

    ba = ba_ref[...]
    beta_out[...] = _sigmoid(ba[:, 0:LANES])
    g_col = -jnp.exp(alog_r[...]) * _softplus(ba[:, LANES:2 * LANES] + dtb_r[...])
    rr = lax.broadcasted_iota(I32, (ts, ts), 0)
    cc = lax.broadcasted_iota(I32, (ts, ts), 1)
    same = (rr // C_CHUNK) == (cc // C_CHUNK)
    gcc_out[...] = _dot((same & (rr >= cc)).astype(F32), g_col, HIGHEST)
    g_row = -jnp.exp(alog_c[...]) * _softplus(bat_ref[C_VHEADS:2 * C_VHEADS, :] + dtb_c[...])
    gcr_out[...] = _dot(g_row, (same & (rr <= cc)).astype(F32), HIGHEST)


def _gdn_pre(p_main, ba, bat, conv_w, a_log, dt_bias, bn, s, *, ts=256):
    t = p_main.shape[0]
    nt = s // ts
    qkv_w = conv_w.shape[1]
    pad_r = lambda z: jnp.pad(z.astype(F32), (0, LANES - C_VHEADS)).reshape(1, LANES)
    col = lambda z: z.astype(F32).reshape(C_VHEADS, 1)
    full = lambda shape: pl.BlockSpec(shape, lambda b, i: (0,) * len(shape))
    hm = lambda nh: pl.BlockSpec((nh, ts, LANES), lambda b, i: (0, b * nt + i, 0))
    return pl.pallas_call(
        functools.partial(_gdn_pre_body, ts),
        out_shape=[jax.ShapeDtypeStruct((C_KHEADS, t, LANES), F32),
                   jax.ShapeDtypeStruct((C_KHEADS, t, LANES), F32),
                   jax.ShapeDtypeStruct((C_VHEADS, t, LANES), F32),
                   jax.ShapeDtypeStruct((t, LANES), F32),
                   jax.ShapeDtypeStruct((t, LANES), F32),
                   jax.ShapeDtypeStruct((C_VHEADS, t), F32)],
        grid=(bn, nt),
        in_specs=[pl.BlockSpec((ts, qkv_w), lambda b, i: (b * nt + i, 0)),
                  full((C_CONV, qkv_w)),
                  pl.BlockSpec((ts, 2 * LANES), lambda b, i: (b * nt + i, 0)),
                  pl.BlockSpec((2 * C_VHEADS, ts), lambda b, i: (0, b * nt + i)),
                  full((1, LANES)), full((1, LANES)), full((C_VHEADS, 1)), full((C_VHEADS, 1))],
        out_specs=[hm(C_KHEADS), hm(C_KHEADS), hm(C_VHEADS),
                   pl.BlockSpec((ts, LANES), lambda b, i: (b * nt + i, 0)),
                   pl.BlockSpec((ts, LANES), lambda b, i: (b * nt + i, 0)),
                   pl.BlockSpec((C_VHEADS, ts), lambda b, i: (0, b * nt + i))],
        scratch_shapes=[pltpu.VMEM((8, qkv_w), F32)],
        compiler_params=_params("parallel", "arbitrary"),
        name="gdn_pre",
    )(p_main, conv_w, ba, bat, pad_r(a_log), pad_r(dt_bias), col(a_log), col(dt_bias))


def _gdn_chunk_body(tb, q_ref, k_ref, v_ref, z_ref, beta_ref, gcc_ref, gcr_ref, ng_ref, o_ref, state):
    c = C_CHUNK
    n = 2 * c
    hk = pl.program_id(1)

    @pl.when(pl.program_id(2) == 0)
    def _():
        state[...] = jnp.zeros_like(state)

    row = lax.broadcasted_iota(I32, (n, n), 0)
    col = lax.broadcasted_iota(I32, (n, n), 1)
    same = (row // c) == (col // c)
    strict = same & (row > col)
    incl = same & (row >= col)
    lane = lax.broadcasted_iota(I32, (1, LANES), 1)
    top = lax.broadcasted_iota(I32, (n, 1), 0) < c
    ng = ng_ref[...]

    def pick(x, hv):
        return jnp.sum(jnp.where(lane == hv, x, 0.0), axis=1, keepdims=True)

    def chunk(ci, _):
        rs = pl.ds(pl.multiple_of(ci * c, c), c)
        q, k = q_ref[rs, :], k_ref[rs, :]
        v = jnp.concatenate([v_ref[0, rs, :], v_ref[1, rs, :]], axis=0)
        bt, gc = beta_ref[rs, :], gcc_ref[rs, :]
        beta = jnp.concatenate([pick(bt, 2 * hk), pick(bt, 2 * hk + 1)], axis=0)
        gcs = jnp.concatenate([pick(gc, 2 * hk), pick(gc, 2 * hk + 1)], axis=0)
        gcr = gcr_ref[pl.ds(ci, 1), :]
        k2 = jnp.concatenate([k, k], axis=0)
        q2 = jnp.concatenate([q, q], axis=0)
        dec = jnp.exp(jnp.where(incl, gcs - gcr, -jnp.inf))
        kk = _bdot_nt(k2, k2)
        qk = _bdot_nt(q2, k2)
        a_blk = jnp.where(strict, kk * beta * dec, 0.0)
        attn = jnp.where(incl, qk * dec, 0.0)
        tinv = _tri_inv(a_blk, c)
        eg = jnp.exp(gcs)
        rhs = jnp.concatenate([v * beta, k2 * (beta * eg)], axis=1)
        sol = _bdot(tinv, rhs)
        u, w = sol[:, 0:LANES], sol[:, LANES:2 * LANES]
        qd = q2 * eg
        gl0, gl1 = gcs[c - 1:c, :], gcs[n - 1:n, :]
        kd = k2 * jnp.exp(jnp.where(top, gl0, gl1) - gcs)
        s0, s1 = state[0], state[1]
        ws0 = _bdot(jnp.concatenate([w[0:c], qd[0:c]], axis=0), s0)
        ws1 = _bdot(jnp.concatenate([w[c:n], qd[c:n]], axis=0), s1)
        v_new = u - jnp.concatenate([ws0[0:c], ws1[0:c]], axis=0)
        o = jnp.concatenate([ws0[c:n], ws1[c:n]], axis=0) + _bdot(attn, v_new)
        state[0] = s0 * jnp.exp(gl0) + _bdot_tn(kd[0:c], v_new[0:c])
        state[1] = s1 * jnp.exp(gl1) + _bdot_tn(kd[c:n], v_new[c:n])
        on = o * lax.rsqrt(jnp.mean(o * o, axis=1, keepdims=True) + C_NORM_EPS) * ng
        o_ref[rs, 0:LANES] = on[0:c] * _silu(z_ref[rs, 0:LANES])
        o_ref[rs, LANES:2 * LANES] = on[c:n] * _silu(z_ref[rs, LANES:2 * LANES])
        return 0

    lax.fori_loop(0, tb // c, chunk, 0)


def _gdn_chunk(qh, kh, vh, p_main, beta_c, gc_c, gc_r, norm_g, bn, s, *, tb=512):
    t = qh.shape[1]
    nt = s // tb
    z_off = (2 * C_KHEADS + C_VHEADS) // 2
    gcr = gc_r.reshape(C_KHEADS, 2, t // C_CHUNK, C_CHUNK).transpose(0, 2, 1, 3).reshape(
        C_KHEADS, t // C_CHUNK, 2 * C_CHUNK)
    tile = lambda f: pl.BlockSpec((None, tb, LANES), f)
    return pl.pallas_call(
        functools.partial(_gdn_chunk_body, tb),
        out_shape=jax.ShapeDtypeStruct((t, C_VHEADS * LANES), F32),
        grid=(bn, C_KHEADS, nt),
        in_specs=[tile(lambda b, h, i: (h, b * nt + i, 0)),
                  tile(lambda b, h, i: (h, b * nt + i, 0)),
                  pl.BlockSpec((2, tb, LANES), lambda b, h, i: (h, b * nt + i, 0)),
                  pl.BlockSpec((tb, 2 * LANES), lambda b, h, i: (b * nt + i, z_off + h)),
                  pl.BlockSpec((tb, LANES), lambda b, h, i: (b * nt + i, 0)),
                  pl.BlockSpec((tb, LANES), lambda b, h, i: (b * nt + i, 0)),
                  pl.BlockSpec((None, tb // C_CHUNK, 2 * C_CHUNK), lambda b, h, i: (h, b * nt + i, 0)),
                  pl.BlockSpec((1, LANES), lambda b, h, i: (0, 0))],
        out_specs=pl.BlockSpec((tb, 2 * LANES), lambda b, h, i: (b * nt + i, h)),
        scratch_shapes=[pltpu.VMEM((2, C_DK, LANES), F32)],
        compiler_params=_params("parallel", "parallel", "arbitrary"),
        name="gdn_chunk",
    )(qh, kh, vh, p_main, beta_c, gc_c, gcr, norm_g.reshape(1, LANES))


def _attn_rwkv_layer(x2, bn, s, w_in, a_lambda, a_norm_g, b_mix, b_w0, b_w2, b_a0, b_a2, b_g2, b_k_k, b_k_a, b_r_k,
                     b_lnx_g, b_lnx_b, w_out, ln_g, ln_b, layer_idx):
    a_cols = 3 * A_HEADS * A_VDIM
    pa = _matmul(x2, w_in[:, :a_cols])
    pb = _matmul(x2, w_in[:, a_cols:], tn=256)
    ya = _diff_attention(pa, a_lambda, a_norm_g, bn, s, layer_idx)
    pre = _rwkv_pre(pb, b_mix, b_w0, b_w2, b_a0, b_a2, b_g2, b_k_k, b_k_a, b_r_k, bn, s)
    yb = _rwkv_scan(*pre, b_lnx_g, b_lnx_b, bn, s)
    wa = A_HEADS * A_VDIM
    return _matmul_res_ln([ya, yb], [w_out[:wa], w_out[wa:]], x2, ln_g, ln_b)


def _gdn_layer(x2, bn, s, w_in, conv_w, a_log, dt_bias, norm_g, w_out, ln_g, ln_b):
    qkv_w = conv_w.shape[1]
    main_w = qkv_w + C_VHEADS * LANES
    w_b = w_in[:, main_w:main_w + C_VHEADS]
    w_a = w_in[:, main_w + C_VHEADS:main_w + 2 * C_VHEADS]
    pad = lambda w: jnp.pad(w, ((0, 0), (0, LANES - C_VHEADS)))
    p_main = _matmul(x2, w_in[:, :main_w])
    ba = _matmul(x2, jnp.concatenate([pad(w_b), pad(w_a)], axis=1), tn=2 * LANES)
    bat = _matmul_t(jnp.concatenate([w_b, w_a], axis=1).T, x2)
    qh, kh, vh, beta_c, gc_c, gc_r = _gdn_pre(p_main, ba, bat, conv_w, a_log, dt_bias, bn, s)
    o = _gdn_chunk(qh, kh, vh, p_main, beta_c, gc_c, gc_r, norm_g, bn, s)
    return _matmul_res_ln([o], [w_out], x2, ln_g, ln_b)


def kernel(x, l0_w_in, l0_a_lambda, l0_a_norm_g, l0_b_mix, l0_b_w0, l0_b_w2, l0_b_a0, l0_b_a2, l0_b_g2, l0_b_k_k, l0_b_k_a, l0_b_r_k, l0_b_lnx_g, l0_b_lnx_b, l0_w_out, l0_ln1_g, l0_ln1_b, l0_router_w, l0_router_b, l0_exp_gate, l0_exp_up, l0_exp_down, l0_sh_gate, l0_sh_up, l0_sh_down, l0_ln2_g, l0_ln2_b, l1_w_in, l1_conv_w, l1_a_log, l1_dt_bias, l1_norm_g, l1_w_out, l1_ln1_g, l1_ln1_b, l1_router_w, l1_router_b, l1_exp_gate, l1_exp_up, l1_exp_down, l1_sh_gate, l1_sh_up, l1_sh_down, l1_ln2_g, l1_ln2_b):
    bn, s, d = x.shape
    x2 = x.reshape(bn * s, d)
    x2, x8 = _attn_rwkv_layer(x2, bn, s, l0_w_in, l0_a_lambda, l0_a_norm_g, l0_b_mix, l0_b_w0, l0_b_w2, l0_b_a0,
                              l0_b_a2, l0_b_g2, l0_b_k_k, l0_b_k_a, l0_b_r_k, l0_b_lnx_g, l0_b_lnx_b, l0_w_out,
                              l0_ln1_g, l0_ln1_b, 0)
    x2 = _moe_ln(x2, x8, l0_router_w, l0_router_b, l0_exp_gate, l0_exp_up, l0_exp_down, l0_sh_gate, l0_sh_up,
                 l0_sh_down, l0_ln2_g, l0_ln2_b)
    x2, x8 = _gdn_layer(x2, bn, s, l1_w_in, l1_conv_w, l1_a_log, l1_dt_bias, l1_norm_g, l1_w_out, l1_ln1_g,
                        l1_ln1_b)
    x2 = _moe_ln(x2, x8, l1_router_w, l1_router_b, l1_exp_gate, l1_exp_up, l1_exp_down, l1_sh_gate, l1_sh_up,
                 l1_sh_down, l1_ln2_g, l1_ln2_b)
    return x2.reshape(bn, s, d)
```

```python
import functools
import math

import jax
import jax.numpy as jnp
import numpy as np
from jax import lax
from jax.experimental import pallas as pl
from jax.experimental.pallas import tpu as pltpu

F32 = jnp.float32
BF16 = jnp.bfloat16
I32 = jnp.int32
HIGHEST = lax.Precision.HIGHEST

LANES = 128
TILE_ROWS = 8
VMEM_LIMIT = 56 * 1024 * 1024

DEPTH = 2
DN_ALPHA = (2.0 * DEPTH) ** 0.25
LN_EPS = 1e-5

A_HEADS = 4
A_VDIM = 128
A_QKDIM = 64
A_NORM_EPS = 1e-5

B_HEAD = 64
B_WIDTH = 512
B_LNX_EPS = 64e-5
B_CHUNK = 64

C_DK = 128
C_KHEADS = 8
C_VHEADS = 16
C_CONV = 4
C_CHUNK = 64
C_NORM_EPS = 1e-6

N_EXPERTS = 64
TOP_K = 8
N_GROUPS = 8
TOPK_GROUPS = 4
ROUTE_SCALE = 2.5
MOE_ROWS = 256


def _params(*sem):
    return pltpu.CompilerParams(dimension_semantics=sem, vmem_limit_bytes=VMEM_LIMIT)


def _dot(a, b, precision=None):
    return jnp.dot(a, b, preferred_element_type=F32, precision=precision)


def _dot_nt(a, b, precision=None):
    return lax.dot_general(a, b, (((1,), (1,)), ((), ())), preferred_element_type=F32, precision=precision)


def _dot_tn(a, b, precision=None):
    return lax.dot_general(a, b, (((0,), (0,)), ((), ())), preferred_element_type=F32, precision=precision)


def _split(x):
    hi = x.astype(BF16)
    return hi, (x - hi.astype(F32)).astype(BF16)


def _dot_sel(sel, x):
    hi, lo = _split(x)
    return _dot(sel, hi) + _dot(sel, lo)


def _dot_by_sel(x, sel):
    hi, lo = _split(x)
    return _dot(hi, sel) + _dot(lo, sel)


def _dot3(a, b):
    ah, al = _split(a)
    bh, bl = _split(b)
    return _dot(ah, bh) + (_dot(ah, bl) + _dot(al, bh))


def _bdot(a, b):
    return _dot(a.astype(BF16), b.astype(BF16))


def _bdot_nt(a, b):
    return _dot_nt(a.astype(BF16), b.astype(BF16))


def _bdot_tn(a, b):
    return _dot_tn(a.astype(BF16), b.astype(BF16))


def _sigmoid(x):
    return 1.0 / (1.0 + jnp.exp(-x))


def _silu(x):
    return x * _sigmoid(x)


def _softplus(x):
    return jnp.maximum(x, 0.0) + jnp.log(1.0 + jnp.exp(-jnp.abs(x)))


def _mm_body(x_ref, w_ref, o_ref):
    o_ref[...] = _dot(x_ref[...].astype(BF16), w_ref[...]).astype(o_ref.dtype)


def _matmul(x, w, *, tm=512, tn=512, out_dtype=F32):
    m, k = x.shape
    n = w.shape[1]
    tn = min(tn, n)
    assert m % tm == 0 and n % tn == 0
    return pl.pallas_call(
        _mm_body,
        out_shape=jax.ShapeDtypeStruct((m, n), out_dtype),
        grid=(m // tm, n // tn),
        in_specs=[pl.BlockSpec((tm, k), lambda i, j: (i, 0)),
                  pl.BlockSpec((k, tn), lambda i, j: (0, j))],
        out_specs=pl.BlockSpec((tm, tn), lambda i, j: (i, j)),
        compiler_params=_params("parallel", "arbitrary"),
        name="matmul",
    )(x, w.astype(BF16))


def _mm_nt_body(w_ref, x_ref, o_ref):
    o_ref[...] = _dot_nt(w_ref[...], x_ref[...].astype(BF16))


def _matmul_t(wt, x, *, tm=512):
    n, k = wt.shape
    m = x.shape[0]
    return pl.pallas_call(
        _mm_nt_body,
        out_shape=jax.ShapeDtypeStruct((n, m), F32),
        grid=(m // tm,),
        in_specs=[pl.BlockSpec((n, k), lambda i: (0, 0)),
                  pl.BlockSpec((tm, k), lambda i: (i, 0))],
        out_specs=pl.BlockSpec((n, tm), lambda i: (0, i)),
        compiler_params=_params("parallel"),
        name="matmul_t",
    )(wt.astype(BF16), x)


def _layer_norm_rows(y, g, b):
    mu = jnp.mean(y, axis=-1, keepdims=True)
    d = y - mu
    var = jnp.mean(d * d, axis=-1, keepdims=True)
    return d * lax.rsqrt(var + LN_EPS) * g + b


def _mm_ln_body(n_parts, tm, *refs):
    parts = refs[:n_parts]
    ws = refs[n_parts:2 * n_parts]
    res_ref, g_ref, b_ref, o_ref, o8_ref = refs[2 * n_parts:]
    acc = DN_ALPHA * res_ref[...]
    for p_ref, w_ref in zip(parts, ws):
        acc = acc + _dot(p_ref[...].astype(BF16), w_ref[...])
    y = _layer_norm_rows(acc, g_ref[...], b_ref[...])
    o_ref[...] = y
    _to_tile_rows(o8_ref, y, tm)


def _matmul_res_ln(parts, ws, res, g, b, *, tm=256):
    m, d = res.shape
    assert d == TILE_ROWS * LANES
    n_parts = len(parts)
    in_specs = [pl.BlockSpec((tm, p.shape[1]), lambda i: (i, 0)) for p in parts]
    in_specs += [pl.BlockSpec(w.shape, lambda i: (0, 0)) for w in ws]
    in_specs += [pl.BlockSpec((tm, d), lambda i: (i, 0)),
                 pl.BlockSpec((1, d), lambda i: (0, 0)),
                 pl.BlockSpec((1, d), lambda i: (0, 0))]
    return pl.pallas_call(
        functools.partial(_mm_ln_body, n_parts, tm),
        out_shape=[jax.ShapeDtypeStruct((m, d), F32), jax.ShapeDtypeStruct((m * TILE_ROWS, LANES), F32)],
        grid=(m // tm,),
        in_specs=in_specs,
        out_specs=[pl.BlockSpec((tm, d), lambda i: (i, 0)), pl.BlockSpec((tm * TILE_ROWS, LANES), lambda i: (i, 0))],
        compiler_params=_params("parallel"),
        name="matmul_res_ln",
    )(*parts, *[w.astype(BF16) for w in ws], res, g.reshape(1, d), b.reshape(1, d))


def _attn_body(seq_blocks, tq, lam_init, slopes_ref, lam_ref, g_ref, q_ref, k_ref, v_ref, o_ref):
    h = pl.program_id(1)
    qi = pl.program_id(2)
    slope = slopes_ref[h]
    lp = lam_ref[...]
    lam = (jnp.exp(jnp.sum(lp[0:1] * lp[1:2], axis=1, keepdims=True))
           - jnp.exp(jnp.sum(lp[2:3] * lp[3:4], axis=1, keepdims=True)) + lam_init)
    lane = lax.broadcasted_iota(I32, (1, A_VDIM), 1)
    q = q_ref[...] * (A_QKDIM ** -0.5)
    q0 = jnp.where(lane < A_QKDIM, q, 0.0).astype(BF16)
    q1 = jnp.where(lane >= A_QKDIM, q, 0.0).astype(BF16)
    rel = (lax.broadcasted_iota(I32, (tq, tq), 0) - lax.broadcasted_iota(I32, (tq, tq), 1))

    def step(j, carry):
        m0, l0, a0, m1, l1, a1 = carry
        kb = k_ref[pl.ds(pl.multiple_of(j * tq, tq), tq), :].astype(BF16)
        vb = v_ref[pl.ds(pl.multiple_of(j * tq, tq), tq), :].astype(BF16)
        dist = rel + (qi - j) * tq
        bias = slope * dist.astype(F32)
        keep = dist >= 0

        def one(qc, m, l, a):
            s = jnp.where(keep, _dot_nt(qc, kb) - bias, -jnp.inf)
            m_new = jnp.maximum(m, jnp.max(s, axis=1, keepdims=True))
            alpha = jnp.exp(m - m_new)
            p = jnp.exp(s - m_new)
            l_new = alpha * l + jnp.sum(p, axis=1, keepdims=True)
            a_new = alpha * a + _dot(p.astype(BF16), vb)
            return m_new, l_new, a_new

        m0, l0, a0 = one(q0, m0, l0, a0)
        m1, l1, a1 = one(q1, m1, l1, a1)
        return m0, l0, a0, m1, l1, a1

    neg = jnp.full((tq, 1), -jnp.inf, F32)
    zero1 = jnp.zeros((tq, 1), F32)
    zero = jnp.zeros((tq, A_VDIM), F32)
    _, l0, a0, _, l1, a1 = lax.fori_loop(0, qi + 1, step, (neg, zero1, zero, neg, zero1, zero))
    o = a0 / l0 - lam * (a1 / l1)
    ms = jnp.mean(o * o, axis=1, keepdims=True)
    o_ref[...] = o * lax.rsqrt(ms + A_NORM_EPS) * g_ref[...] * (1.0 - lam_init)


def _diff_attention(pa, lam_params, norm_g, bn, s, layer_idx, *, tq=256):
    t = pa.shape[0]
    nq = s // tq
    lam_init = 0.8 - 0.6 * math.exp(-0.3 * layer_idx)
    slopes = jnp.asarray(np.array([2.0 ** (-8.0 * (i + 1) / A_HEADS) for i in range(A_HEADS)], dtype=np.float32))
    body = functools.partial(_attn_body, nq, tq, lam_init)
    return pl.pallas_call(
        body,
        out_shape=jax.ShapeDtypeStruct((t, A_HEADS * A_VDIM), F32),
        grid=(bn, A_HEADS, nq),
        in_specs=[pl.BlockSpec(memory_space=pltpu.SMEM),
                  pl.BlockSpec((4, A_QKDIM), lambda b, h, i: (0, 0)),
                  pl.BlockSpec((1, A_VDIM), lambda b, h, i: (0, 0)),
                  pl.BlockSpec((tq, A_VDIM), lambda b, h, i: (b * nq + i, h)),
                  pl.BlockSpec((s, A_VDIM), lambda b, h, i: (b, A_HEADS + h)),
                  pl.BlockSpec((s, A_VDIM), lambda b, h, i: (b, 2 * A_HEADS + h))],
        out_specs=pl.BlockSpec((tq, A_VDIM), lambda b, h, i: (b * nq + i, h)),
        compiler_params=_params("parallel", "parallel", "arbitrary"),
        name="diff_attention",
    )(slopes, lam_params.astype(F32), norm_g.reshape(1, A_VDIM), pa, pa, pa)


def _tri_inv(a, blk):
    n = a.shape[0]
    row = lax.broadcasted_iota(I32, (n, n), 0)
    col = lax.broadcasted_iota(I32, (n, n), 1)
    eye = (row == col).astype(F32)

    def same(m):
        return (row // m) == (col // m)

    d = jnp.where(same(8), a, 0.0)
    d2 = _bdot(d, d)
    d4 = _bdot(d2, d2)
    t = _bdot(_bdot(eye - d, eye + d2), eye + d4)
    m = 8
    while m < blk:
        e = jnp.where(same(2 * m) & jnp.logical_not(same(m)), a, 0.0)
        t = t - _bdot(t, _bdot(e, t))
        m *= 2
    return t


def _group_sum(x, ones_blk):
    return _dot_by_sel(x, ones_blk)


def _rwkv_pre_body(ts, pb_ref, mix_ref, w0_ref, w2_ref, a0_ref, a2_ref, g2_ref, kk_ref, ka_ref, rk_ref, ones_ref,
                   r_out, k_out, v_out, kap_out, b_out, ld_out, g_out, bv_out, carry):
    i = pl.program_id(1)

    @pl.when(i == 0)
    def _():
        carry[...] = jnp.zeros_like(carry)

    cur = pb_ref[...]
    rolled = pltpu.roll(cur, 1, 0)
    first = lax.broadcasted_iota(I32, (ts, 1), 0) == 0
    prev = jnp.where(first, carry[0:1, :], rolled)
    carry[0:1, :] = cur[ts - 1:ts, :]
    x = cur + (prev - cur) * mix_ref[...]
    r = x[:, 0:B_WIDTH]
    k = x[:, B_WIDTH:2 * B_WIDTH]
    v = x[:, 2 * B_WIDTH:3 * B_WIDTH]
    xwa = x[:, 3 * B_WIDTH:3 * B_WIDTH + LANES]
    xg = x[:, 3 * B_WIDTH + LANES:3 * B_WIDTH + 2 * LANES]
    w_log = -_softplus(-(w0_ref[...] + _dot(jnp.tanh(xwa).astype(BF16), w2_ref[...]))) - 0.5
    a = _sigmoid(a0_ref[...] + _dot(xwa.astype(BF16), a2_ref[...]))
    g = _dot(_sigmoid(xg).astype(BF16), g2_ref[...])
    ones_blk = ones_ref[...]
    kkr = k * kk_ref[...]
    kp = k * (1.0 + (a - 1.0) * ka_ref[...])
    rkr = r * kp * rk_ref[...]
    for c in range(B_WIDTH // LANES):
        sl = slice(c * LANES, (c + 1) * LANES)
        kn = kkr[:, sl]
        kap = kn * lax.rsqrt(_group_sum(kn * kn, ones_blk) + 1e-6)
        kap_out[:, sl] = kap
        b_out[:, sl] = kap * a[:, sl]
        bv_out[:, sl] = _group_sum(rkr[:, sl], ones_blk) * v[:, sl]
    r_out[...] = r
    k_out[...] = kp
    v_out[...] = v
    ld_out[...] = -jnp.exp(w_log)
    g_out[...] = g


def _head_group_ones():
    idx = np.arange(LANES) // B_HEAD
    return jnp.asarray((idx[:, None] == idx[None, :]).astype(np.float32)).astype(BF16)


def _rwkv_pre(pb, mix, w0, w2, a0, a2, g2, k_k, k_a, r_k, bn, s, *, ts=256):
    t, cols = pb.shape
    nt = s // ts
    lora = w2.shape[0]
    w2p = jnp.concatenate([w2, jnp.zeros_like(w2)], axis=0).astype(BF16)
    a2p = jnp.concatenate([jnp.zeros_like(a2), a2], axis=0).astype(BF16)
    assert lora * 2 == LANES
    row = lambda z: z.reshape(1, -1)
    full = lambda shape: pl.BlockSpec(shape, lambda b, i: (0, 0))
    out = jax.ShapeDtypeStruct((t, B_WIDTH), F32)
    tile = pl.BlockSpec((ts, B_WIDTH), lambda b, i: (b * nt + i, 0))
    return pl.pallas_call(
        functools.partial(_rwkv_pre_body, ts),
        out_shape=[out] * 8,
        grid=(bn, nt),
        in_specs=[pl.BlockSpec((ts, cols), lambda b, i: (b * nt + i, 0)),
                  full((1, cols)), full((1, B_WIDTH)), full((LANES, B_WIDTH)), full((1, B_WIDTH)),
                  full((LANES, B_WIDTH)), full((LANES, B_WIDTH)), full((1, B_WIDTH)), full((1, B_WIDTH)),
                  full((1, B_WIDTH)), full((LANES, LANES))],
        out_specs=[tile] * 8,
        scratch_shapes=[pltpu.VMEM((8, cols), F32)],
        compiler_params=_params("parallel", "arbitrary"),
        name="rwkv_pre",
    )(pb, row(mix), row(w0), w2p, row(a0), a2p, g2.astype(BF16), row(k_k), row(k_a), row(r_k), _head_group_ones())


def _rwkv_scan_body(c, r_ref, k_ref, v_ref, kap_ref, b_ref, ld_ref, g_ref, bv_ref, lng_ref, lnb_ref, ones_ref,
                    o_ref, state):
    @pl.when(pl.program_id(1) == 0)
    def _():
        state[...] = jnp.zeros_like(state)

    n = 2 * c
    row = lax.broadcasted_iota(I32, (n, n), 0)
    col = lax.broadcasted_iota(I32, (n, n), 1)
    same = (row // c) == (col // c)
    strict = same & (row > col)
    incl = same & (row >= col)
    trow = lax.broadcasted_iota(I32, (c, c), 0)
    tcol = lax.broadcasted_iota(I32, (c, c), 1)
    tril = (trow >= tcol).astype(BF16)
    head0 = lax.broadcasted_iota(I32, (1, LANES), 1) < B_HEAD
    ones_blk = ones_ref[...]

    def sel(z):
        return jnp.where(head0, z[0:c], z[c:n])

    def dup(z):
        return jnp.concatenate([z, z], axis=0)

    for p in range(B_WIDTH // LANES):
        sl = slice(p * LANES, (p + 1) * LANES)
        ld = ld_ref[:, sl]
        r, k, v, kap, b = r_ref[:, sl], k_ref[:, sl], v_ref[:, sl], kap_ref[:, sl], b_ref[:, sl]
        cum = _dot_sel(tril, ld)
        gam = jnp.exp(cum)
        einv = jnp.exp(-cum)
        kt = kap * jnp.exp(cum - ld)
        rt = r * gam
        kh = k * einv
        bh = b * einv
        g_last = gam[c - 1:c, :]
        x = jnp.concatenate([jnp.where(head0, kt, 0.0), jnp.where(head0, 0.0, kt),
                             jnp.where(head0, rt, 0.0), jnp.where(head0, 0.0, rt)], axis=0)
        y = jnp.concatenate([kh, kh, bh, bh], axis=0)
        m2 = _bdot_nt(x, y)
        a_kk = jnp.where(strict, m2[0:n, 0:n], 0.0)
        a_bk = jnp.where(strict, m2[0:n, n:2 * n], 0.0)
        b_rk = jnp.where(incl, m2[n:2 * n, 0:n], 0.0)
        b_rb = jnp.where(incl, m2[n:2 * n, n:2 * n], 0.0)
        tinv = _tri_inv(a_bk, c)
        s_old = state[p]
        ps = _bdot_nt(jnp.concatenate([kt, rt], axis=0), s_old)
        vv = dup(v)
        rhs = ps[0:c] + sel(_bdot(a_kk, vv))
        u = sel(_bdot(tinv, dup(rhs)))
        yc = ps[c:n] + sel(_bdot(b_rk, vv) - _bdot(b_rb, dup(u)))
        upd = _bdot_tn(jnp.concatenate([v, u], axis=0),
                       jnp.concatenate([kh * g_last, -(bh * g_last)], axis=0))
        state[p] = s_old * g_last + jnp.where(same, upd, 0.0)
        mu = _group_sum(yc, ones_blk) * (1.0 / B_HEAD)
        dy = yc - mu
        var = _group_sum(dy * dy, ones_blk) * (1.0 / B_HEAD)
        yn = dy * lax.rsqrt(var + B_LNX_EPS) * lng_ref[:, sl] + lnb_ref[:, sl]
        o_ref[:, sl] = (yn + bv_ref[:, sl]) * g_ref[:, sl]


def _rwkv_scan(r, k, v, kap, b, ld, g, bv, lnx_g, lnx_b, bn, s, *, c=B_CHUNK):
    t = r.shape[0]
    nc = s // c
    tile = pl.BlockSpec((c, B_WIDTH), lambda bi, i: (bi * nc + i, 0))
    vec = pl.BlockSpec((1, B_WIDTH), lambda bi, i: (0, 0))
    return pl.pallas_call(
        functools.partial(_rwkv_scan_body, c),
        out_shape=jax.ShapeDtypeStruct((t, B_WIDTH), F32),
        grid=(bn, nc),
        in_specs=[tile] * 8 + [vec, vec, pl.BlockSpec((LANES, LANES), lambda bi, i: (0, 0))],
        out_specs=tile,
        scratch_shapes=[pltpu.VMEM((B_WIDTH // LANES, LANES, LANES), F32)],
        compiler_params=_params("parallel", "arbitrary"),
        name="rwkv_scan",
    )(r, k, v, kap, b, ld, g, bv, lnx_g.reshape(1, -1), lnx_b.reshape(1, -1), _head_group_ones())


def _first_index_of_max(vals, idx, sentinel):
    m = jnp.max(vals, axis=0, keepdims=True)
    first = jnp.min(jnp.where(vals == m, idx, sentinel), axis=0, keepdims=True)
    return m, first


def _router_body(tm, w_ref, bias_ref, h_ref, eidx_ref, gate_ref, rank_ref, count_ref, carry):
    @pl.when(pl.program_id(0) == 0)
    def _():
        carry[...] = jnp.zeros_like(carry)

    per_group = N_EXPERTS // N_GROUPS
    logits = _dot_nt(w_ref[...], h_ref[...], HIGHEST)
    scores = _sigmoid(logits)
    choice = scores + bias_ref[...]
    sub = lax.broadcasted_iota(I32, (per_group, tm), 0)
    grp_rows = []
    for g in range(N_GROUPS):
        cg = choice[g * per_group:(g + 1) * per_group, :]
        m1, first = _first_index_of_max(cg, sub, per_group)
        m2 = jnp.max(jnp.where(sub == first, -jnp.inf, cg), axis=0, keepdims=True)
        grp_rows.append(m1 + m2)
    gs = jnp.concatenate(grp_rows, axis=0)
    gidx = lax.broadcasted_iota(I32, (N_GROUPS, tm), 0)
    gsel = jnp.zeros((N_GROUPS, tm), jnp.bool_)
    for _ in range(TOPK_GROUPS):
        _, first = _first_index_of_max(gs, gidx, N_GROUPS)
        pick = gidx == first
        gsel = gsel | pick
        gs = jnp.where(pick, -jnp.inf, gs)
    ch = jnp.concatenate(
        [jnp.where(gsel[g:g + 1, :], choice[g * per_group:(g + 1) * per_group, :], -jnp.inf)
         for g in range(N_GROUPS)], axis=0)
    eidx = lax.broadcasted_iota(I32, (N_EXPERTS, tm), 0)
    idx_rows, score_rows, picks = [], [], []
    for _ in range(TOP_K):
        _, first = _first_index_of_max(ch, eidx, N_EXPERTS)
        pick = eidx == first
        picks.append(pick)
        idx_rows.append(first)
        score_rows.append(jnp.sum(jnp.where(pick, scores, 0.0), axis=0, keepdims=True))
        ch = jnp.where(pick, -jnp.inf, ch)
    sc = jnp.concatenate(score_rows, axis=0)
    eidx_ref[...] = jnp.concatenate(idx_rows, axis=0)
    gate_ref[...] = sc / jnp.sum(sc, axis=0, keepdims=True) * ROUTE_SCALE

    onehot = picks[0].astype(F32)
    for pick in picks[1:]:
        onehot = onehot + pick.astype(F32)
    before = (lax.broadcasted_iota(I32, (tm, tm), 0) < lax.broadcasted_iota(I32, (tm, tm), 1)).astype(BF16)
    seen = carry[...] + _dot(onehot.astype(BF16), before)
    rank_ref[...] = jnp.concatenate(
        [jnp.sum(jnp.where(pick, seen, 0.0), axis=0, keepdims=True) for pick in picks], axis=0).astype(I32)
    total = carry[...] + jnp.sum(onehot, axis=1, keepdims=True)
    carry[...] = total
    count_ref[...] = total.astype(I32)


def _router(h, router_w, router_b, *, tm=512):
    t, d = h.shape
    row_blk = pl.BlockSpec((TOP_K, tm), lambda i: (0, i))
    return pl.pallas_call(
        functools.partial(_router_body, tm),
        out_shape=[jax.ShapeDtypeStruct((TOP_K, t), I32), jax.ShapeDtypeStruct((TOP_K, t), F32),
                   jax.ShapeDtypeStruct((TOP_K, t), I32), jax.ShapeDtypeStruct((N_EXPERTS, 1), I32)],
        grid=(t // tm,),
        in_specs=[pl.BlockSpec((N_EXPERTS, d), lambda i: (0, 0)),
                  pl.BlockSpec((N_EXPERTS, 1), lambda i: (0, 0)),
                  pl.BlockSpec((tm, d), lambda i: (i, 0))],
        out_specs=[row_blk, row_blk, row_blk, pl.BlockSpec((N_EXPERTS, 1), lambda i: (0, 0))],
        scratch_shapes=[pltpu.VMEM((N_EXPERTS, 1), F32)],
        compiler_params=_params("arbitrary"),
        name="moe_router",
    )(router_w.T, router_b.reshape(N_EXPERTS, 1), h)


def _block_plan(counts, n_assign):
    counts = counts.reshape(N_EXPERTS)
    padded = (counts + MOE_ROWS - 1) // MOE_ROWS * MOE_ROWS
    pend = jnp.cumsum(padded)
    pstart = (pend - padded).astype(I32)
    n_blocks = (n_assign + N_EXPERTS * (MOE_ROWS - 1) + MOE_ROWS - 1) // MOE_ROWS
    starts = jnp.arange(n_blocks, dtype=I32) * MOE_ROWS
    block_e = jnp.minimum(jnp.sum(starts[:, None] >= pend[None, :], axis=1), N_EXPERTS - 1).astype(I32)
    live = jnp.clip(counts[block_e] - (starts - pstart[block_e]), 0, MOE_ROWS).astype(I32)
    return pstart, block_e, live, n_blocks


def _tile_major(idx_t, tm):
    k, t = idx_t.shape
    return idx_t.reshape(k, t // tm, tm).transpose(1, 0, 2).reshape(t // tm, 1, k * tm)


def _rows_of(tok):
    return pl.ds(pl.multiple_of(tok * TILE_ROWS, TILE_ROWS), TILE_ROWS)


def _to_tile_rows(ref, y, n):
    for j in range(TILE_ROWS):
        ref[pl.ds(j, n, stride=TILE_ROWS), :] = y[:, j * LANES:(j + 1) * LANES]


def _dispatch_body(tm, n_blocks, pstart_ref, live_ref, eidx_ref, rank_ref, h8_ref, xs_hbm, zeros, sem, zsem):
    @pl.when(pl.program_id(0) == 0)
    def _():
        zeros[...] = jnp.zeros_like(zeros)
        rows = MOE_ROWS * TILE_ROWS

        def fill(wait):
            def body(b, _):
                @pl.when(live_ref[b] < MOE_ROWS)
                def _():
                    cp = pltpu.make_async_copy(zeros, xs_hbm.at[pl.ds(pl.multiple_of(b * rows, rows), rows), :],
                                               zsem)
                    cp.wait() if wait else cp.start()
                return 0
            return body

        lax.fori_loop(0, n_blocks, fill(False), 0)
        lax.fori_loop(0, n_blocks, fill(True), 0)

    def issue(tok, _):
        src = h8_ref.at[_rows_of(tok), :]
        for k in range(TOP_K):
            slot = pstart_ref[eidx_ref[0, 0, k * tm + tok]] + rank_ref[0, 0, k * tm + tok]
            pltpu.make_async_copy(src, xs_hbm.at[_rows_of(slot), :], sem).start()
        return 0

    lax.fori_loop(0, tm, issue, 0)
    for _ in range(TOP_K):
        pltpu.make_async_copy(h8_ref, xs_hbm.at[pl.ds(0, tm * TILE_ROWS), :], sem).wait()


def _dispatch(h8, pstart, live, eidx_t, rank_t, n_blocks, *, tm=256):
    t = eidx_t.shape[1]
    grid_spec = pltpu.PrefetchScalarGridSpec(
        num_scalar_prefetch=2,
        grid=(t // tm,),
        in_specs=[pl.BlockSpec((1, 1, TOP_K * tm), lambda i, ps, lv: (i, 0, 0), memory_space=pltpu.SMEM),
                  pl.BlockSpec((1, 1, TOP_K * tm), lambda i, ps, lv: (i, 0, 0), memory_space=pltpu.SMEM),
                  pl.BlockSpec((tm * TILE_ROWS, LANES), lambda i, ps, lv: (i, 0))],
        out_specs=pl.BlockSpec(memory_space=pl.ANY),
        scratch_shapes=[pltpu.VMEM((MOE_ROWS * TILE_ROWS, LANES), F32),
                        pltpu.SemaphoreType.DMA(()), pltpu.SemaphoreType.DMA(())],
    )
    return pl.pallas_call(
        functools.partial(_dispatch_body, tm, n_blocks),
        out_shape=jax.ShapeDtypeStruct((n_blocks * MOE_ROWS * TILE_ROWS, LANES), F32),
        grid_spec=grid_spec,
        compiler_params=_params("arbitrary"),
        name="moe_dispatch",
    )(pstart, live, _tile_major(eidx_t, tm), _tile_major(rank_t, tm), h8)


def _expert_body(be_ref, live_ref, x_ref, wg_ref, wu_ref, wd_ref, o_ref):
    n_live = live_ref[pl.program_id(0)]

    @pl.when(n_live > 0)
    def _():
        x = jnp.concatenate([x_ref[pl.ds(j, MOE_ROWS, stride=TILE_ROWS), :] for j in range(TILE_ROWS)], axis=1)
        x = x.astype(BF16)
        hid = _silu(_dot(x, wg_ref[...])) * _dot(x, wu_ref[...])
        _to_tile_rows(o_ref, _dot(hid.astype(BF16), wd_ref[...]), MOE_ROWS)

    @pl.when(n_live == 0)
    def _():
        o_ref[...] = jnp.zeros_like(o_ref)


def _expert_blocks(xs, block_e, live, n_blocks, wg, wu, wd):
    d, de = wg.shape[1], wg.shape[2]
    rows = MOE_ROWS * TILE_ROWS
    grid_spec = pltpu.PrefetchScalarGridSpec(
        num_scalar_prefetch=2,
        grid=(n_blocks,),
        in_specs=[pl.BlockSpec((rows, LANES), lambda i, be, lv: (i, 0)),
                  pl.BlockSpec((None, d, de), lambda i, be, lv: (be[i], 0, 0)),
                  pl.BlockSpec((None, d, de), lambda i, be, lv: (be[i], 0, 0)),
                  pl.BlockSpec((None, de, d), lambda i, be, lv: (be[i], 0, 0))],
        out_specs=pl.BlockSpec((rows, LANES), lambda i, be, lv: (i, 0)),
    )
    return pl.pallas_call(
        _expert_body,
        out_shape=jax.ShapeDtypeStruct((n_blocks * rows, LANES), F32),
        grid_spec=grid_spec,
        compiler_params=_params("arbitrary"),
        name="moe_experts",
    )(block_e, live, xs, wg.astype(BF16), wu.astype(BF16), wd.astype(BF16))


def _combine_body(n_tiles, tm, pstart_ref, cur_e, cur_r, nxt_e, nxt_r, ys_hbm, h_ref, gate_ref,
                  sg_ref, su_ref, sd_ref, g_ref, b_ref, o_ref, buf, sem):
    i = pl.program_id(0)
    slot = i % 2

    def gather(e_ref, r_ref, dst, dsem):
        def issue(tok, _):
            for k in range(TOP_K):
                src = pstart_ref[e_ref[0, 0, k * tm + tok]] + r_ref[0, 0, k * tm + tok]
                pltpu.make_async_copy(ys_hbm.at[_rows_of(src), :], dst.at[_rows_of(k * tm + tok), :], dsem).start()
            return 0
        lax.fori_loop(0, tm, issue, 0)

    @pl.when(i == 0)
    def _():
        gather(cur_e, cur_r, buf.at[0], sem.at[0])

    @pl.when(i + 1 < n_tiles)
    def _():
        gather(nxt_e, nxt_r, buf.at[1 - slot], sem.at[1 - slot])

    h = h_ref[...]
    hb = h.astype(BF16)
    hid = _silu(_dot(hb, sg_ref[...])) * _dot(hb, su_ref[...])
    acc = DN_ALPHA * h + _dot(hid.astype(BF16), sd_ref[...])
    pltpu.make_async_copy(ys_hbm.at[pl.ds(0, TOP_K * tm * TILE_ROWS), :], buf.at[slot], sem.at[slot]).wait()
    gates = gate_ref[...]
    cols = []
    for j in range(TILE_ROWS):
        part = None
        for k in range(TOP_K):
            rows = buf[slot, pl.ds(k * tm * TILE_ROWS + j, tm, stride=TILE_ROWS), :]
            term = gates[:, k:k + 1] * rows
            part = term if part is None else part + term
        cols.append(part)
    o_ref[...] = _layer_norm_rows(acc + jnp.concatenate(cols, axis=1), g_ref[...], b_ref[...])


def _combine_ln(h, ys, pstart, eidx_t, rank_t, gate_t, sh_gate, sh_up, sh_down, ln_g, ln_b, *, tm=256):
    t, d = h.shape
    de = sh_gate.shape[1]
    n_tiles = t // tm
    e3, r3 = _tile_major(eidx_t, tm), _tile_major(rank_t, tm)
    cur = pl.BlockSpec((1, 1, TOP_K * tm), lambda i, ps: (i, 0, 0), memory_space=pltpu.SMEM)
    nxt = pl.BlockSpec((1, 1, TOP_K * tm), lambda i, ps: (jnp.minimum(i + 1, n_tiles - 1), 0, 0),
                       memory_space=pltpu.SMEM)
    full = lambda shape: pl.BlockSpec(shape, lambda i, ps: (0, 0))
    grid_spec = pltpu.PrefetchScalarGridSpec(
        num_scalar_prefetch=1,
        grid=(n_tiles,),
        in_specs=[cur, cur, nxt, nxt,
                  pl.BlockSpec(memory_space=pl.ANY),
                  pl.BlockSpec((tm, d), lambda i, ps: (i, 0)),
                  pl.BlockSpec((tm, TOP_K), lambda i, ps: (i, 0)),
                  full((d, de)), full((d, de)), full((de, d)), full((1, d)), full((1, d))],
        out_specs=pl.BlockSpec((tm, d), lambda i, ps: (i, 0)),
        scratch_shapes=[pltpu.VMEM((2, TOP_K * tm * TILE_ROWS, LANES), F32), pltpu.SemaphoreType.DMA((2,))],
    )
    return pl.pallas_call(
        functools.partial(_combine_body, n_tiles, tm),
        out_shape=jax.ShapeDtypeStruct((t, d), F32),
        grid_spec=grid_spec,
        compiler_params=_params("arbitrary"),
        name="moe_combine_ln",
    )(pstart, e3, r3, e3, r3, ys, h, gate_t.T, sh_gate.astype(BF16), sh_up.astype(BF16), sh_down.astype(BF16),
      ln_g.reshape(1, d), ln_b.reshape(1, d))


def _moe_ln(h, h8, router_w, router_b, exp_gate, exp_up, exp_down, sh_gate, sh_up, sh_down, ln_g, ln_b):
    eidx_t, gate_t, rank_t, counts = _router(h, router_w, router_b)
    pstart, block_e, live, n_blocks = _block_plan(counts, eidx_t.size)
    xs = _dispatch(h8, pstart, live, eidx_t, rank_t, n_blocks)
    ys = _expert_blocks(xs, block_e, live, n_blocks, exp_gate, exp_up, exp_down)
    return _combine_ln(h, ys, pstart, eidx_t, rank_t, gate_t, sh_gate, sh_up, sh_down, ln_g, ln_b)


def _gdn_pre_body(ts, qkv_ref, w_ref, ba_ref, bat_ref, alog_r, dtb_r, alog_c, dtb_c,
                  q_out, k_out, v_out, beta_out, gcc_out, gcr_out, carry):
    i = pl.program_id(1)

    @pl.when(i == 0)
    def _():
        carry[...] = jnp.zeros_like(carry)

    sub8 = lax.broadcasted_iota(I32, (8, 1), 0)
    n_qk = C_KHEADS
    for hb in range(2 * C_KHEADS + C_VHEADS):
        sl = slice(hb * LANES, (hb + 1) * LANES)
        cur = qkv_ref[:, sl]
        tail = carry[:, sl]
        w = w_ref[:, sl]
        acc = cur * w[C_CONV - 1:C_CONV, :]
        for j in range(1, C_CONV):
            rolled = pltpu.roll(cur, j, 0)
            head = jnp.where(sub8 < j, pltpu.roll(tail, j, 0), rolled[0:8])
            shifted = jnp.concatenate([head, rolled[8:]], axis=0)
            acc = acc + shifted * w[C_CONV - 1 - j:C_CONV - j, :]
        carry[:, sl] = cur[ts - 8:ts, :]
        y = _silu(acc)
        if hb < 2 * n_qk:
            y = y * lax.rsqrt(jnp.sum(y * y, axis=1, keepdims=True) + 1e-6)
            if hb < n_qk:
                q_out[hb] = y * (C_DK ** -0.5)
            else:
                k_out[hb - n_qk] = y
        else:
            v_out[hb - 2 * n_qk] = y

    ba = ba_ref[...]
    beta_out[...] = _sigmoid(ba[:, 0:LANES])
    g_col = -jnp.exp(alog_r[...]) * _softplus(ba[:, LANES:2 * LANES] + dtb_r[...])
    rr = lax.broadcasted_iota(I32, (ts, ts), 0)
    cc = lax.broadcasted_iota(I32, (ts, ts), 1)
    same = (rr // C_CHUNK) == (cc // C_CHUNK)
    gcc_out[...] = _dot((same & (rr >= cc)).astype(F32), g_col, HIGHEST)
    g_row = -jnp.exp(alog_c[...]) * _softplus(bat_ref[C_VHEADS:2 * C_VHEADS, :] + dtb_c[...])
    gcr_out[...] = _dot(g_row, (same & (rr <= cc)).astype(F32), HIGHEST)


def _gdn_pre(p_main, ba, bat, conv_w, a_log, dt_bias, bn, s, *, ts=256):
    t = p_main.shape[0]
    nt = s // ts
    qkv_w = conv_w.shape[1]
    pad_r = lambda z: jnp.pad(z.astype(F32), (0, LANES - C_VHEADS)).reshape(1, LANES)
    col = lambda z: z.astype(F32).reshape(C_VHEADS, 1)
    full = lambda shape: pl.BlockSpec(shape, lambda b, i: (0,) * len(shape))
    hm = lambda nh: pl.BlockSpec((nh, ts, LANES), lambda b, i: (0, b * nt + i, 0))
    return pl.pallas_call(
        functools.partial(_gdn_pre_body, ts),
        out_shape=[jax.ShapeDtypeStruct((C_KHEADS, t, LANES), F32),
                   jax.ShapeDtypeStruct((C_KHEADS, t, LANES), F32),
                   jax.ShapeDtypeStruct((C_VHEADS, t, LANES), F32),
                   jax.ShapeDtypeStruct((t, LANES), F32),
                   jax.ShapeDtypeStruct((t, LANES), F32),
                   jax.ShapeDtypeStruct((C_VHEADS, t), F32)],
        grid=(bn, nt),
        in_specs=[pl.BlockSpec((ts, qkv_w), lambda b, i: (b * nt + i, 0)),
                  full((C_CONV, qkv_w)),
                  pl.BlockSpec((ts, 2 * LANES), lambda b, i: (b * nt + i, 0)),
                  pl.BlockSpec((2 * C_VHEADS, ts), lambda b, i: (0, b * nt + i)),
                  full((1, LANES)), full((1, LANES)), full((C_VHEADS, 1)), full((C_VHEADS, 1))],
        out_specs=[hm(C_KHEADS), hm(C_KHEADS), hm(C_VHEADS),
                   pl.BlockSpec((ts, LANES), lambda b, i: (b * nt + i, 0)),
                   pl.BlockSpec((ts, LANES), lambda b, i: (b * nt + i, 0)),
                   pl.BlockSpec((C_VHEADS, ts), lambda b, i: (0, b * nt + i))],
        scratch_shapes=[pltpu.VMEM((8, qkv_w), F32)],
        compiler_params=_params("parallel", "arbitrary"),
        name="gdn_pre",
    )(p_main, conv_w, ba, bat, pad_r(a_log), pad_r(dt_bias), col(a_log), col(dt_bias))


def _gdn_chunk_body(tb, gq, q_ref, k_ref, v_ref, z_ref, beta_ref, gcc_ref, gcr_ref, ng_ref, o_ref, state):
    c = C_CHUNK
    n = 2 * c
    hk0 = pl.program_id(1) * gq

    @pl.when(pl.program_id(2) == 0)
    def _():
        state[...] = jnp.zeros_like(state)

    row = lax.broadcasted_iota(I32, (n, n), 0)
    col = lax.broadcasted_iota(I32, (n, n), 1)
    same = (row // c) == (col // c)
    strict = same & (row > col)
    incl = same & (row >= col)
    lane = lax.broadcasted_iota(I32, (1, LANES), 1)
    top = lax.broadcasted_iota(I32, (n, 1), 0) < c
    ng = ng_ref[...]

    def pick(x, hv):
        return jnp.sum(jnp.where(lane == hv, x, 0.0), axis=1, keepdims=True)

    def one_head(ci, rs, hh):
        hk = hk0 + hh
        q, k = q_ref[hh, rs, :], k_ref[hh, rs, :]
        v = jnp.concatenate([v_ref[2 * hh, rs, :], v_ref[2 * hh + 1, rs, :]], axis=0)
        bt, gc = beta_ref[rs, :], gcc_ref[rs, :]
        beta = jnp.concatenate([pick(bt, 2 * hk), pick(bt, 2 * hk + 1)], axis=0)
        gcs = jnp.concatenate([pick(gc, 2 * hk), pick(gc, 2 * hk + 1)], axis=0)
        gcr = gcr_ref[hh, pl.ds(ci, 1), :]
        k2 = jnp.concatenate([k, k], axis=0)
        q2 = jnp.concatenate([q, q], axis=0)
        dec = jnp.exp(jnp.where(incl, gcs - gcr, -jnp.inf))
        kk = _bdot_nt(k2, k2)
        qk = _bdot_nt(q2, k2)
        a_blk = jnp.where(strict, kk * beta * dec, 0.0)
        attn = jnp.where(incl, qk * dec, 0.0)
        tinv = _tri_inv(a_blk, c)
        eg = jnp.exp(gcs)
        rhs = jnp.concatenate([v * beta, k2 * (beta * eg)], axis=1)
        sol = _bdot(tinv, rhs)
        u, w = sol[:, 0:LANES], sol[:, LANES:2 * LANES]
        qd = q2 * eg
        gl0, gl1 = gcs[c - 1:c, :], gcs[n - 1:n, :]
        kd = k2 * jnp.exp(jnp.where(top, gl0, gl1) - gcs)
        s0, s1 = state[2 * hh], state[2 * hh + 1]
        ws0 = _bdot(jnp.concatenate([w[0:c], qd[0:c]], axis=0), s0)
        ws1 = _bdot(jnp.concatenate([w[c:n], qd[c:n]], axis=0), s1)
        v_new = u - jnp.concatenate([ws0[0:c], ws1[0:c]], axis=0)
        o = jnp.concatenate([ws0[c:n], ws1[c:n]], axis=0) + _bdot(attn, v_new)
        state[2 * hh] = s0 * jnp.exp(gl0) + _bdot_tn(kd[0:c], v_new[0:c])
        state[2 * hh + 1] = s1 * jnp.exp(gl1) + _bdot_tn(kd[c:n], v_new[c:n])
        on = o * lax.rsqrt(jnp.mean(o * o, axis=1, keepdims=True) + C_NORM_EPS) * ng
        c0 = 2 * hh * LANES
        o_ref[rs, c0:c0 + LANES] = on[0:c] * _silu(z_ref[rs, c0:c0 + LANES])
        o_ref[rs, c0 + LANES:c0 + 2 * LANES] = on[c:n] * _silu(z_ref[rs, c0 + LANES:c0 + 2 * LANES])

    def chunk(ci, _):
        rs = pl.ds(pl.multiple_of(ci * c, c), c)
        for hh in range(gq):
            one_head(ci, rs, hh)
        return 0

    lax.fori_loop(0, tb // c, chunk, 0)


def _gdn_chunk(qh, kh, vh, p_main, beta_c, gc_c, gc_r, norm_g, bn, s, *, tb=512, gq=4):
    t = qh.shape[1]
    nt = s // tb
    z_off = (2 * C_KHEADS + C_VHEADS) // 2
    gcr = gc_r.reshape(C_KHEADS, 2, t // C_CHUNK, C_CHUNK).transpose(0, 2, 1, 3).reshape(
        C_KHEADS, t // C_CHUNK, 2 * C_CHUNK)
    tile = lambda f: pl.BlockSpec((gq, tb, LANES), f)
    return pl.pallas_call(
        functools.partial(_gdn_chunk_body, tb, gq),
        out_shape=jax.ShapeDtypeStruct((t, C_VHEADS * LANES), F32),
        grid=(bn, C_KHEADS // gq, nt),
        in_specs=[tile(lambda b, h, i: (h, b * nt + i, 0)),
                  tile(lambda b, h, i: (h, b * nt + i, 0)),
                  pl.BlockSpec((2 * gq, tb, LANES), lambda b, h, i: (h, b * nt + i, 0)),
                  pl.BlockSpec((tb, 2 * gq * LANES), lambda b, h, i: (b * nt + i, z_off // gq + h)),
                  pl.BlockSpec((tb, LANES), lambda b, h, i: (b * nt + i, 0)),
                  pl.BlockSpec((tb, LANES), lambda b, h, i: (b * nt + i, 0)),
                  pl.BlockSpec((gq, tb // C_CHUNK, 2 * C_CHUNK), lambda b, h, i: (h, b * nt + i, 0)),
                  pl.BlockSpec((1, LANES), lambda b, h, i: (0, 0))],
        out_specs=pl.BlockSpec((tb, 2 * gq * LANES), lambda b, h, i: (b * nt + i, h)),
        scratch_shapes=[pltpu.VMEM((2 * gq, C_DK, LANES), F32)],
        compiler_params=_params("parallel", "parallel", "arbitrary"),
        name="gdn_chunk",
    )(qh, kh, vh, p_main, beta_c, gc_c, gcr, norm_g.reshape(1, LANES))


def _attn_rwkv_layer(x2, bn, s, w_in, a_lambda, a_norm_g, b_mix, b_w0, b_w2, b_a0, b_a2, b_g2, b_k_k, b_k_a, b_r_k,
                     b_lnx_g, b_lnx_b, w_out, ln_g, ln_b, layer_idx):
    a_cols = 3 * A_HEADS * A_VDIM
    pa = _matmul(x2, w_in[:, :a_cols], tn=a_cols)
    pb = _matmul(x2, w_in[:, a_cols:], tn=(w_in.shape[1] - a_cols) // 2)
    ya = _diff_attention(pa, a_lambda, a_norm_g, bn, s, layer_idx)
    pre = _rwkv_pre(pb, b_mix, b_w0, b_w2, b_a0, b_a2, b_g2, b_k_k, b_k_a, b_r_k, bn, s)
    yb = _rwkv_scan(*pre, b_lnx_g, b_lnx_b, bn, s)
    wa = A_HEADS * A_VDIM
    return _matmul_res_ln([ya, yb], [w_out[:wa], w_out[wa:]], x2, ln_g, ln_b)


def _gdn_layer(x2, bn, s, w_in, conv_w, a_log, dt_bias, norm_g, w_out, ln_g, ln_b):
    qkv_w = conv_w.shape[1]
    main_w = qkv_w + C_VHEADS * LANES
    w_b = w_in[:, main_w:main_w + C_VHEADS]
    w_a = w_in[:, main_w + C_VHEADS:main_w + 2 * C_VHEADS]
    pad = lambda w: jnp.pad(w, ((0, 0), (0, LANES - C_VHEADS)))
    p_main = _matmul(x2, w_in[:, :main_w], tn=main_w // 4)
    ba = _matmul(x2, jnp.concatenate([pad(w_b), pad(w_a)], axis=1), tn=2 * LANES)
    bat = _matmul_t(jnp.concatenate([w_b, w_a], axis=1).T, x2)
    qh, kh, vh, beta_c, gc_c, gc_r = _gdn_pre(p_main, ba, bat, conv_w, a_log, dt_bias, bn, s)
    o = _gdn_chunk(qh, kh, vh, p_main, beta_c, gc_c, gc_r, norm_g, bn, s)
    return _matmul_res_ln([o], [w_out], x2, ln_g, ln_b)


def kernel(x, l0_w_in, l0_a_lambda, l0_a_norm_g, l0_b_mix, l0_b_w0, l0_b_w2, l0_b_a0, l0_b_a2, l0_b_g2, l0_b_k_k, l0_b_k_a, l0_b_r_k, l0_b_lnx_g, l0_b_lnx_b, l0_w_out, l0_ln1_g, l0_ln1_b, l0_router_w, l0_router_b, l0_exp_gate, l0_exp_up, l0_exp_down, l0_sh_gate, l0_sh_up, l0_sh_down, l0_ln2_g, l0_ln2_b, l1_w_in, l1_conv_w, l1_a_log, l1_dt_bias, l1_norm_g, l1_w_out, l1_ln1_g, l1_ln1_b, l1_router_w, l1_router_b, l1_exp_gate, l1_exp_up, l1_exp_down, l1_sh_gate, l1_sh_up, l1_sh_down, l1_ln2_g, l1_ln2_b):
    bn, s, d = x.shape
    x2 = x.reshape(bn * s, d)
    x2, x8 = _attn_rwkv_layer(x2, bn, s, l0_w_in, l0_a_lambda, l0_a_norm_g, l0_b_mix, l0_b_w0, l0_b_w2, l0_b_a0,
                              l0_b_a2, l0_b_g2, l0_b_k_k, l0_b_k_a, l0_b_r_k, l0_b_lnx_g, l0_b_lnx_b, l0_w_out,
                              l0_ln1_g, l0_ln1_b, 0)
    x2 = _moe_ln(x2, x8, l0_router_w, l0_router_b, l0_exp_gate, l0_exp_up, l0_exp_down, l0_sh_gate, l0_sh_up,
                 l0_sh_down, l0_ln2_g, l0_ln2_b)
    x2, x8 = _gdn_layer(x2, bn, s, l1_w_in, l1_conv_w, l1_a_log, l1_dt_bias, l1_norm_g, l1_w_out, l1_ln1_g,
                        l1_ln1_b)
    x2 = _moe_ln(x2, x8, l1_router_w, l1_router_b, l1_exp_gate, l1_exp_up, l1_exp_down, l1_sh_gate, l1_sh_up,
                 l1_sh_down, l1_ln2_g, l1_ln2_b)
    return x2.reshape(bn, s, d)
```

```python
import functools
import math

import jax
import jax.numpy as jnp
import numpy as np
from jax import lax
from jax.experimental import pallas as pl
from jax.experimental.pallas import tpu as pltpu

F32 = jnp.float32
BF16 = jnp.bfloat16
I32 = jnp.int32
HIGHEST = lax.Precision.HIGHEST

LANES = 128
TILE_ROWS = 8
VMEM_LIMIT = 56 * 1024 * 1024

DEPTH = 2
DN_ALPHA = (2.0 * DEPTH) ** 0.25
LN_EPS = 1e-5

A_HEADS = 4
A_VDIM = 128
A_QKDIM = 64
A_NORM_EPS = 1e-5

B_HEAD = 64
B_WIDTH = 512
B_LNX_EPS = 64e-5
B_CHUNK = 64

C_DK = 128
C_KHEADS = 8
C_VHEADS = 16
C_CONV = 4
C_CHUNK = 64
C_NORM_EPS = 1e-6

N_EXPERTS = 64
TOP_K = 8
N_GROUPS = 8
TOPK_GROUPS = 4
ROUTE_SCALE = 2.5
MOE_ROWS = 256


def _params(*sem):
    return pltpu.CompilerParams(dimension_semantics=sem, vmem_limit_bytes=VMEM_LIMIT)


def _dot(a, b, precision=None):
    return jnp.dot(a, b, preferred_element_type=F32, precision=precision)


def _dot_nt(a, b, precision=None):
    return lax.dot_general(a, b, (((1,), (1,)), ((), ())), preferred_element_type=F32, precision=precision)


def _dot_tn(a, b, precision=None):
    return lax.dot_general(a, b, (((0,), (0,)), ((), ())), preferred_element_type=F32, precision=precision)


def _split(x):
    hi = x.astype(BF16)
    return hi, (x - hi.astype(F32)).astype(BF16)


def _dot_sel(sel, x):
    hi, lo = _split(x)
    return _dot(sel, hi) + _dot(sel, lo)


def _dot_by_sel(x, sel):
    hi, lo = _split(x)
    return _dot(hi, sel) + _dot(lo, sel)


def _dot3(a, b):
    ah, al = _split(a)
    bh, bl = _split(b)
    return _dot(ah, bh) + (_dot(ah, bl) + _dot(al, bh))


def _bdot(a, b):
    return _dot(a.astype(BF16), b.astype(BF16))


def _bdot_nt(a, b):
    return _dot_nt(a.astype(BF16), b.astype(BF16))


def _bdot_tn(a, b):
    return _dot_tn(a.astype(BF16), b.astype(BF16))


def _sigmoid(x):
    return 1.0 / (1.0 + jnp.exp(-x))


def _silu(x):
    return x * _sigmoid(x)


def _softplus(x):
    return jnp.maximum(x, 0.0) + jnp.log(1.0 + jnp.exp(-jnp.abs(x)))


def _mm_body(x_ref, w_ref, o_ref):
    o_ref[...] = _dot(x_ref[...].astype(BF16), w_ref[...]).astype(o_ref.dtype)


def _matmul(x, w, *, tm=512, tn=512, out_dtype=F32):
    m, k = x.shape
    n = w.shape[1]
    tn = min(tn, n)
    assert m % tm == 0 and n % tn == 0
    return pl.pallas_call(
        _mm_body,
        out_shape=jax.ShapeDtypeStruct((m, n), out_dtype),
        grid=(m // tm, n // tn),
        in_specs=[pl.BlockSpec((tm, k), lambda i, j: (i, 0)),
                  pl.BlockSpec((k, tn), lambda i, j: (0, j))],
        out_specs=pl.BlockSpec((tm, tn), lambda i, j: (i, j)),
        compiler_params=_params("parallel", "arbitrary"),
        name="matmul",
    )(x, w.astype(BF16))


def _mm_nt_body(w_ref, x_ref, o_ref):
    o_ref[...] = _dot_nt(w_ref[...], x_ref[...].astype(BF16))


def _matmul_t(wt, x, *, tm=512):
    n, k = wt.shape
    m = x.shape[0]
    return pl.pallas_call(
        _mm_nt_body,
        out_shape=jax.ShapeDtypeStruct((n, m), F32),
        grid=(m // tm,),
        in_specs=[pl.BlockSpec((n, k), lambda i: (0, 0)),
                  pl.BlockSpec((tm, k), lambda i: (i, 0))],
        out_specs=pl.BlockSpec((n, tm), lambda i: (0, i)),
        compiler_params=_params("parallel"),
        name="matmul_t",
    )(wt.astype(BF16), x)


def _layer_norm_rows(y, g, b):
    mu = jnp.mean(y, axis=-1, keepdims=True)
    d = y - mu
    var = jnp.mean(d * d, axis=-1, keepdims=True)
    return d * lax.rsqrt(var + LN_EPS) * g + b


def _mm_ln_body(n_parts, tm, *refs):
    parts = refs[:n_parts]
    ws = refs[n_parts:2 * n_parts]
    res_ref, g_ref, b_ref, o_ref, o8_ref = refs[2 * n_parts:]
    acc = DN_ALPHA * res_ref[...]
    for p_ref, w_ref in zip(parts, ws):
        acc = acc + _dot(p_ref[...].astype(BF16), w_ref[...])
    y = _layer_norm_rows(acc, g_ref[...], b_ref[...])
    o_ref[...] = y
    _to_tile_rows(o8_ref, y, tm)


def _matmul_res_ln(parts, ws, res, g, b, *, tm=256):
    m, d = res.shape
    assert d == TILE_ROWS * LANES
    n_parts = len(parts)
    in_specs = [pl.BlockSpec((tm, p.shape[1]), lambda i: (i, 0)) for p in parts]
    in_specs += [pl.BlockSpec(w.shape, lambda i: (0, 0)) for w in ws]
    in_specs += [pl.BlockSpec((tm, d), lambda i: (i, 0)),
                 pl.BlockSpec((1, d), lambda i: (0, 0)),
                 pl.BlockSpec((1, d), lambda i: (0, 0))]
    return pl.pallas_call(
        functools.partial(_mm_ln_body, n_parts, tm),
        out_shape=[jax.ShapeDtypeStruct((m, d), F32), jax.ShapeDtypeStruct((m * TILE_ROWS, LANES), F32)],
        grid=(m // tm,),
        in_specs=in_specs,
        out_specs=[pl.BlockSpec((tm, d), lambda i: (i, 0)), pl.BlockSpec((tm * TILE_ROWS, LANES), lambda i: (i, 0))],
        compiler_params=_params("parallel"),
        name="matmul_res_ln",
    )(*parts, *[w.astype(BF16) for w in ws], res, g.reshape(1, d), b.reshape(1, d))


def _attn_body(seq_blocks, tq, lam_init, slopes_ref, lam_ref, g_ref, q_ref, k_ref, v_ref, o_ref):
    h = pl.program_id(1)
    qi = pl.program_id(2)
    slope = slopes_ref[h]
    lp = lam_ref[...]
    lam = (jnp.exp(jnp.sum(lp[0:1] * lp[1:2], axis=1, keepdims=True))
           - jnp.exp(jnp.sum(lp[2:3] * lp[3:4], axis=1, keepdims=True)) + lam_init)
    lane = lax.broadcasted_iota(I32, (1, A_VDIM), 1)
    q = q_ref[...] * (A_QKDIM ** -0.5)
    q0 = jnp.where(lane < A_QKDIM, q, 0.0).astype(BF16)
    q1 = jnp.where(lane >= A_QKDIM, q, 0.0).astype(BF16)
    rel = (lax.broadcasted_iota(I32, (tq, tq), 0) - lax.broadcasted_iota(I32, (tq, tq), 1))

    def step(j, carry):
        m0, l0, a0, m1, l1, a1 = carry
        kb = k_ref[pl.ds(pl.multiple_of(j * tq, tq), tq), :].astype(BF16)
        vb = v_ref[pl.ds(pl.multiple_of(j * tq, tq), tq), :].astype(BF16)
        dist = rel + (qi - j) * tq
        bias = slope * dist.astype(F32)
        keep = dist >= 0

        def one(qc, m, l, a):
            s = jnp.where(keep, _dot_nt(qc, kb) - bias, -jnp.inf)
            m_new = jnp.maximum(m, jnp.max(s, axis=1, keepdims=True))
            alpha = jnp.exp(m - m_new)
            p = jnp.exp(s - m_new)
            l_new = alpha * l + jnp.sum(p, axis=1, keepdims=True)
            a_new = alpha * a + _dot(p.astype(BF16), vb)
            return m_new, l_new, a_new

        m0, l0, a0 = one(q0, m0, l0, a0)
        m1, l1, a1 = one(q1, m1, l1, a1)
        return m0, l0, a0, m1, l1, a1

    neg = jnp.full((tq, 1), -jnp.inf, F32)
    zero1 = jnp.zeros((tq, 1), F32)
    zero = jnp.zeros((tq, A_VDIM), F32)
    _, l0, a0, _, l1, a1 = lax.fori_loop(0, qi + 1, step, (neg, zero1, zero, neg, zero1, zero))
    o = a0 / l0 - lam * (a1 / l1)
    ms = jnp.mean(o * o, axis=1, keepdims=True)
    o_ref[...] = o * lax.rsqrt(ms + A_NORM_EPS) * g_ref[...] * (1.0 - lam_init)


def _diff_attention(pa, lam_params, norm_g, bn, s, layer_idx, *, tq=256):
    t = pa.shape[0]
    nq = s // tq
    lam_init = 0.8 - 0.6 * math.exp(-0.3 * layer_idx)
    slopes = jnp.asarray(np.array([2.0 ** (-8.0 * (i + 1) / A_HEADS) for i in range(A_HEADS)], dtype=np.float32))
    body = functools.partial(_attn_body, nq, tq, lam_init)
    return pl.pallas_call(
        body,
        out_shape=jax.ShapeDtypeStruct((t, A_HEADS * A_VDIM), F32),
        grid=(bn, A_HEADS, nq),
        in_specs=[pl.BlockSpec(memory_space=pltpu.SMEM),
                  pl.BlockSpec((4, A_QKDIM), lambda b, h, i: (0, 0)),
                  pl.BlockSpec((1, A_VDIM), lambda b, h, i: (0, 0)),
                  pl.BlockSpec((tq, A_VDIM), lambda b, h, i: (b * nq + i, h)),
                  pl.BlockSpec((s, A_VDIM), lambda b, h, i: (b, A_HEADS + h)),
                  pl.BlockSpec((s, A_VDIM), lambda b, h, i: (b, 2 * A_HEADS + h))],
        out_specs=pl.BlockSpec((tq, A_VDIM), lambda b, h, i: (b * nq + i, h)),
        compiler_params=_params("parallel", "parallel", "arbitrary"),
        name="diff_attention",
    )(slopes, lam_params.astype(F32), norm_g.reshape(1, A_VDIM), pa, pa, pa)


def _tri_inv(a, blk):
    n = a.shape[0]
    row = lax.broadcasted_iota(I32, (n, n), 0)
    col = lax.broadcasted_iota(I32, (n, n), 1)
    eye = (row == col).astype(F32)

    def same(m):
        return (row // m) == (col // m)

    d = jnp.where(same(8), a, 0.0)
    d2 = _bdot(d, d)
    d4 = _bdot(d2, d2)
    t = _bdot(_bdot(eye - d, eye + d2), eye + d4)
    m = 8
    while m < blk:
        e = jnp.where(same(2 * m) & jnp.logical_not(same(m)), a, 0.0)
        t = t - _bdot(t, _bdot(e, t))
        m *= 2
    return t


def _group_sum(x, ones_blk):
    return _dot_by_sel(x, ones_blk)


def _rwkv_pre_body(ts, pb_ref, mix_ref, w0_ref, w2_ref, a0_ref, a2_ref, g2_ref, kk_ref, ka_ref, rk_ref, ones_ref,
                   r_out, k_out, v_out, kap_out, b_out, ld_out, g_out, bv_out, carry):
    i = pl.program_id(1)

    @pl.when(i == 0)
    def _():
        carry[...] = jnp.zeros_like(carry)

    cur = pb_ref[...]
    rolled = pltpu.roll(cur, 1, 0)
    first = lax.broadcasted_iota(I32, (ts, 1), 0) == 0
    prev = jnp.where(first, carry[0:1, :], rolled)
    carry[0:1, :] = cur[ts - 1:ts, :]
    x = cur + (prev - cur) * mix_ref[...]
    r = x[:, 0:B_WIDTH]
    k = x[:, B_WIDTH:2 * B_WIDTH]
    v = x[:, 2 * B_WIDTH:3 * B_WIDTH]
    xwa = x[:, 3 * B_WIDTH:3 * B_WIDTH + LANES]
    xg = x[:, 3 * B_WIDTH + LANES:3 * B_WIDTH + 2 * LANES]
    w_log = -_softplus(-(w0_ref[...] + _dot(jnp.tanh(xwa).astype(BF16), w2_ref[...]))) - 0.5
    a = _sigmoid(a0_ref[...] + _dot(xwa.astype(BF16), a2_ref[...]))
    g = _dot(_sigmoid(xg).astype(BF16), g2_ref[...])
    ones_blk = ones_ref[...]
    kkr = k * kk_ref[...]
    kp = k * (1.0 + (a - 1.0) * ka_ref[...])
    rkr = r * kp * rk_ref[...]
    for c in range(B_WIDTH // LANES):
        sl = slice(c * LANES, (c + 1) * LANES)
        kn = kkr[:, sl]
        kap = kn * lax.rsqrt(_group_sum(kn * kn, ones_blk) + 1e-6)
        kap_out[:, sl] = kap
        b_out[:, sl] = kap * a[:, sl]
        bv_out[:, sl] = _group_sum(rkr[:, sl], ones_blk) * v[:, sl]
    r_out[...] = r
    k_out[...] = kp
    v_out[...] = v
    ld_out[...] = -jnp.exp(w_log)
    g_out[...] = g


def _head_group_ones():
    idx = np.arange(LANES) // B_HEAD
    return jnp.asarray((idx[:, None] == idx[None, :]).astype(np.float32)).astype(BF16)


def _rwkv_pre(pb, mix, w0, w2, a0, a2, g2, k_k, k_a, r_k, bn, s, *, ts=256):
    t, cols = pb.shape
    nt = s // ts
    lora = w2.shape[0]
    w2p = jnp.concatenate([w2, jnp.zeros_like(w2)], axis=0).astype(BF16)
    a2p = jnp.concatenate([jnp.zeros_like(a2), a2], axis=0).astype(BF16)
    assert lora * 2 == LANES
    row = lambda z: z.reshape(1, -1)
    full = lambda shape: pl.BlockSpec(shape, lambda b, i: (0, 0))
    out = jax.ShapeDtypeStruct((t, B_WIDTH), F32)
    tile = pl.BlockSpec((ts, B_WIDTH), lambda b, i: (b * nt + i, 0))
    return pl.pallas_call(
        functools.partial(_rwkv_pre_body, ts),
        out_shape=[out] * 8,
        grid=(bn, nt),
        in_specs=[pl.BlockSpec((ts, cols), lambda b, i: (b * nt + i, 0)),
                  full((1, cols)), full((1, B_WIDTH)), full((LANES, B_WIDTH)), full((1, B_WIDTH)),
                  full((LANES, B_WIDTH)), full((LANES, B_WIDTH)), full((1, B_WIDTH)), full((1, B_WIDTH)),
                  full((1, B_WIDTH)), full((LANES, LANES))],
        out_specs=[tile] * 8,
        scratch_shapes=[pltpu.VMEM((8, cols), F32)],
        compiler_params=_params("parallel", "arbitrary"),
        name="rwkv_pre",
    )(pb, row(mix), row(w0), w2p, row(a0), a2p, g2.astype(BF16), row(k_k), row(k_a), row(r_k), _head_group_ones())


def _rwkv_scan_body(c, nb, r_ref, k_ref, v_ref, kap_ref, b_ref, ld_ref, g_ref, bv_ref, lng_ref, lnb_ref, ones_ref,
                    o_ref, state):
    @pl.when(pl.program_id(1) == 0)
    def _():
        state[...] = jnp.zeros_like(state)

    n = 2 * c
    row = lax.broadcasted_iota(I32, (n, n), 0)
    col = lax.broadcasted_iota(I32, (n, n), 1)
    same = (row // c) == (col // c)
    strict = same & (row > col)
    incl = same & (row >= col)
    trow = lax.broadcasted_iota(I32, (c, c), 0)
    tcol = lax.broadcasted_iota(I32, (c, c), 1)
    tril = (trow >= tcol).astype(BF16)
    head0 = lax.broadcasted_iota(I32, (1, LANES), 1) < B_HEAD
    ones_blk = ones_ref[...]

    def sel(z):
        return jnp.where(head0, z[0:c], z[c:n])

    def dup(z):
        return jnp.concatenate([z, z], axis=0)

    n_pairs = B_WIDTH // LANES
    for bb, p in [(bb, p) for bb in range(nb) for p in range(n_pairs)]:
        sl = slice(p * LANES, (p + 1) * LANES)
        ld = ld_ref[bb, :, sl]
        r, k, v = r_ref[bb, :, sl], k_ref[bb, :, sl], v_ref[bb, :, sl]
        kap, b = kap_ref[bb, :, sl], b_ref[bb, :, sl]
        cum = _dot_sel(tril, ld)
        gam = jnp.exp(cum)
        einv = jnp.exp(-cum)
        kt = kap * jnp.exp(cum - ld)
        rt = r * gam
        kh = k * einv
        bh = b * einv
        g_last = gam[c - 1:c, :]
        x = jnp.concatenate([jnp.where(head0, kt, 0.0), jnp.where(head0, 0.0, kt),
                             jnp.where(head0, rt, 0.0), jnp.where(head0, 0.0, rt)], axis=0)
        y = jnp.concatenate([kh, kh, bh, bh], axis=0)
        m2 = _bdot_nt(x, y)
        a_kk = jnp.where(strict, m2[0:n, 0:n], 0.0)
        a_bk = jnp.where(strict, m2[0:n, n:2 * n], 0.0)
        b_rk = jnp.where(incl, m2[n:2 * n, 0:n], 0.0)
        b_rb = jnp.where(incl, m2[n:2 * n, n:2 * n], 0.0)
        tinv = _tri_inv(a_bk, c)
        s_old = state[bb * n_pairs + p]
        ps = _bdot_nt(jnp.concatenate([kt, rt], axis=0), s_old)
        vv = dup(v)
        rhs = ps[0:c] + sel(_bdot(a_kk, vv))
        u = sel(_bdot(tinv, dup(rhs)))
        yc = ps[c:n] + sel(_bdot(b_rk, vv) - _bdot(b_rb, dup(u)))
        upd = _bdot_tn(jnp.concatenate([v, u], axis=0),
                       jnp.concatenate([kh * g_last, -(bh * g_last)], axis=0))
        state[bb * n_pairs + p] = s_old * g_last + jnp.where(same, upd, 0.0)
        mu = _group_sum(yc, ones_blk) * (1.0 / B_HEAD)
        dy = yc - mu
        var = _group_sum(dy * dy, ones_blk) * (1.0 / B_HEAD)
        yn = dy * lax.rsqrt(var + B_LNX_EPS) * lng_ref[:, sl] + lnb_ref[:, sl]
        o_ref[bb, :, sl] = (yn + bv_ref[bb, :, sl]) * g_ref[bb, :, sl]


def _rwkv_scan(r, k, v, kap, b, ld, g, bv, lnx_g, lnx_b, bn, s, *, c=B_CHUNK, nb=2):
    t = r.shape[0]
    nc = s // c
    nb = math.gcd(nb, bn)
    seq = lambda z: z.reshape(bn, s, B_WIDTH)
    tile = pl.BlockSpec((nb, c, B_WIDTH), lambda bi, i: (bi, i, 0))
    vec = pl.BlockSpec((1, B_WIDTH), lambda bi, i: (0, 0))
    out = pl.pallas_call(
        functools.partial(_rwkv_scan_body, c, nb),
        out_shape=jax.ShapeDtypeStruct((bn, s, B_WIDTH), F32),
        grid=(bn // nb, nc),
        in_specs=[tile] * 8 + [vec, vec, pl.BlockSpec((LANES, LANES), lambda bi, i: (0, 0))],
        out_specs=tile,
        scratch_shapes=[pltpu.VMEM((nb * (B_WIDTH // LANES), LANES, LANES), F32)],
        compiler_params=_params("parallel", "arbitrary"),
        name="rwkv_scan",
    )(seq(r), seq(k), seq(v), seq(kap), seq(b), seq(ld), seq(g), seq(bv),
      lnx_g.reshape(1, -1), lnx_b.reshape(1, -1), _head_group_ones())
    return out.reshape(t, B_WIDTH)


def _first_index_of_max(vals, idx, sentinel):
    m = jnp.max(vals, axis=0, keepdims=True)
    first = jnp.min(jnp.where(vals == m, idx, sentinel), axis=0, keepdims=True)
    return m, first


def _router_body(tm, w_ref, bias_ref, h_ref, eidx_ref, gate_ref, rank_ref, count_ref, carry):
    @pl.when(pl.program_id(0) == 0)
    def _():
        carry[...] = jnp.zeros_like(carry)

    per_group = N_EXPERTS // N_GROUPS
    logits = _dot_nt(w_ref[...], h_ref[...], HIGHEST)
    scores = _sigmoid(logits)
    choice = scores + bias_ref[...]
    sub = lax.broadcasted_iota(I32, (per_group, tm), 0)
    grp_rows = []
    for g in range(N_GROUPS):
        cg = choice[g * per_group:(g + 1) * per_group, :]
        m1, first = _first_index_of_max(cg, sub, per_group)
        m2 = jnp.max(jnp.where(sub == first, -jnp.inf, cg), axis=0, keepdims=True)
        grp_rows.append(m1 + m2)
    gs = jnp.concatenate(grp_rows, axis=0)
    gidx = lax.broadcasted_iota(I32, (N_GROUPS, tm), 0)
    gsel = jnp.zeros((N_GROUPS, tm), jnp.bool_)
    for _ in range(TOPK_GROUPS):
        _, first = _first_index_of_max(gs, gidx, N_GROUPS)
        pick = gidx == first
        gsel = gsel | pick
        gs = jnp.where(pick, -jnp.inf, gs)
    ch = jnp.concatenate(
        [jnp.where(gsel[g:g + 1, :], choice[g * per_group:(g + 1) * per_group, :], -jnp.inf)
         for g in range(N_GROUPS)], axis=0)
    eidx = lax.broadcasted_iota(I32, (N_EXPERTS, tm), 0)
    idx_rows, score_rows, picks = [], [], []
    for _ in range(TOP_K):
        _, first = _first_index_of_max(ch, eidx, N_EXPERTS)
        pick = eidx == first
        picks.append(pick)
        idx_rows.append(first)
        score_rows.append(jnp.sum(jnp.where(pick, scores, 0.0), axis=0, keepdims=True))
        ch = jnp.where(pick, -jnp.inf, ch)
    sc = jnp.concatenate(score_rows, axis=0)
    eidx_ref[...] = jnp.concatenate(idx_rows, axis=0)
    gate_ref[...] = sc / jnp.sum(sc, axis=0, keepdims=True) * ROUTE_SCALE

    onehot = picks[0].astype(F32)
    for pick in picks[1:]:
        onehot = onehot + pick.astype(F32)
    before = (lax.broadcasted_iota(I32, (tm, tm), 0) < lax.broadcasted_iota(I32, (tm, tm), 1)).astype(BF16)
    seen = carry[...] + _dot(onehot.astype(BF16), before)
    rank_ref[...] = jnp.concatenate(
        [jnp.sum(jnp.where(pick, seen, 0.0), axis=0, keepdims=True) for pick in picks], axis=0).astype(I32)
    total = carry[...] + jnp.sum(onehot, axis=1, keepdims=True)
    carry[...] = total
    count_ref[...] = total.astype(I32)


def _router(h, router_w, router_b, *, tm=512):
    t, d = h.shape
    row_blk = pl.BlockSpec((TOP_K, tm), lambda i: (0, i))
    return pl.pallas_call(
        functools.partial(_router_body, tm),
        out_shape=[jax.ShapeDtypeStruct((TOP_K, t), I32), jax.ShapeDtypeStruct((TOP_K, t), F32),
                   jax.ShapeDtypeStruct((TOP_K, t), I32), jax.ShapeDtypeStruct((N_EXPERTS, 1), I32)],
        grid=(t // tm,),
        in_specs=[pl.BlockSpec((N_EXPERTS, d), lambda i: (0, 0)),
                  pl.BlockSpec((N_EXPERTS, 1), lambda i: (0, 0)),
                  pl.BlockSpec((tm, d), lambda i: (i, 0))],
        out_specs=[row_blk, row_blk, row_blk, pl.BlockSpec((N_EXPERTS, 1), lambda i: (0, 0))],
        scratch_shapes=[pltpu.VMEM((N_EXPERTS, 1), F32)],
        compiler_params=_params("arbitrary"),
        name="moe_router",
    )(router_w.T, router_b.reshape(N_EXPERTS, 1), h)


def _block_plan(counts, n_assign):
    counts = counts.reshape(N_EXPERTS)
    padded = (counts + MOE_ROWS - 1) // MOE_ROWS * MOE_ROWS
    pend = jnp.cumsum(padded)
    pstart = (pend - padded).astype(I32)
    n_blocks = (n_assign + N_EXPERTS * (MOE_ROWS - 1) + MOE_ROWS - 1) // MOE_ROWS
    starts = jnp.arange(n_blocks, dtype=I32) * MOE_ROWS
    block_e = jnp.minimum(jnp.sum(starts[:, None] >= pend[None, :], axis=1), N_EXPERTS - 1).astype(I32)
    live = jnp.clip(counts[block_e] - (starts - pstart[block_e]), 0, MOE_ROWS).astype(I32)
    return pstart, block_e, live, n_blocks


def _tile_major(idx_t, tm):
    k, t = idx_t.shape
    return idx_t.reshape(k, t // tm, tm).transpose(1, 0, 2).reshape(t // tm, 1, k * tm)


def _rows_of(tok):
    return pl.ds(pl.multiple_of(tok * TILE_ROWS, TILE_ROWS), TILE_ROWS)


def _to_tile_rows(ref, y, n):
    for j in range(TILE_ROWS):
        ref[pl.ds(j, n, stride=TILE_ROWS), :] = y[:, j * LANES:(j + 1) * LANES]


def _dispatch_body(tm, n_blocks, pstart_ref, live_ref, eidx_ref, rank_ref, h8_ref, xs_hbm, zeros, sem, zsem):
    @pl.when(pl.program_id(0) == 0)
    def _():
        zeros[...] = jnp.zeros_like(zeros)
        rows = MOE_ROWS * TILE_ROWS

        def fill(wait):
            def body(b, _):
                @pl.when(live_ref[b] < MOE_ROWS)
                def _():
                    cp = pltpu.make_async_copy(zeros, xs_hbm.at[pl.ds(pl.multiple_of(b * rows, rows), rows), :],
                                               zsem)
                    cp.wait() if wait else cp.start()
                return 0
            return body

        lax.fori_loop(0, n_blocks, fill(False), 0)
        lax.fori_loop(0, n_blocks, fill(True), 0)

    def issue(tok, _):
        src = h8_ref.at[_rows_of(tok), :]
        for k in range(TOP_K):
            slot = pstart_ref[eidx_ref[0, 0, k * tm + tok]] + rank_ref[0, 0, k * tm + tok]
            pltpu.make_async_copy(src, xs_hbm.at[_rows_of(slot), :], sem).start()
        return 0

    lax.fori_loop(0, tm, issue, 0)
    for _ in range(TOP_K):
        pltpu.make_async_copy(h8_ref, xs_hbm.at[pl.ds(0, tm * TILE_ROWS), :], sem).wait()


def _dispatch(h8, pstart, live, eidx_t, rank_t, n_blocks, *, tm=256):
    t = eidx_t.shape[1]
    grid_spec = pltpu.PrefetchScalarGridSpec(
        num_scalar_prefetch=2,
        grid=(t // tm,),
        in_specs=[pl.BlockSpec((1, 1, TOP_K * tm), lambda i, ps, lv: (i, 0, 0), memory_space=pltpu.SMEM),
                  pl.BlockSpec((1, 1, TOP_K * tm), lambda i, ps, lv: (i, 0, 0), memory_space=pltpu.SMEM),
                  pl.BlockSpec((tm * TILE_ROWS, LANES), lambda i, ps, lv: (i, 0))],
        out_specs=pl.BlockSpec(memory_space=pl.ANY),
        scratch_shapes=[pltpu.VMEM((MOE_ROWS * TILE_ROWS, LANES), F32),
                        pltpu.SemaphoreType.DMA(()), pltpu.SemaphoreType.DMA(())],
    )
    return pl.pallas_call(
        functools.partial(_dispatch_body, tm, n_blocks),
        out_shape=jax.ShapeDtypeStruct((n_blocks * MOE_ROWS * TILE_ROWS, LANES), F32),
        grid_spec=grid_spec,
        compiler_params=_params("arbitrary"),
        name="moe_dispatch",
    )(pstart, live, _tile_major(eidx_t, tm), _tile_major(rank_t, tm), h8)


def _expert_body(be_ref, live_ref, x_ref, wg_ref, wu_ref, wd_ref, o_ref):
    n_live = live_ref[pl.program_id(0)]

    @pl.when(n_live > 0)
    def _():
        x = jnp.concatenate([x_ref[pl.ds(j, MOE_ROWS, stride=TILE_ROWS), :] for j in range(TILE_ROWS)], axis=1)
        x = x.astype(BF16)
        hid = _silu(_dot(x, wg_ref[...])) * _dot(x, wu_ref[...])
        _to_tile_rows(o_ref, _dot(hid.astype(BF16), wd_ref[...]), MOE_ROWS)

    @pl.when(n_live == 0)
    def _():
        o_ref[...] = jnp.zeros_like(o_ref)


def _expert_blocks(xs, block_e, live, n_blocks, wg, wu, wd):
    d, de = wg.shape[1], wg.shape[2]
    rows = MOE_ROWS * TILE_ROWS
    grid_spec = pltpu.PrefetchScalarGridSpec(
        num_scalar_prefetch=2,
        grid=(n_blocks,),
        in_specs=[pl.BlockSpec((rows, LANES), lambda i, be, lv: (i, 0)),
                  pl.BlockSpec((None, d, de), lambda i, be, lv: (be[i], 0, 0)),
                  pl.BlockSpec((None, d, de), lambda i, be, lv: (be[i], 0, 0)),
                  pl.BlockSpec((None, de, d), lambda i, be, lv: (be[i], 0, 0))],
        out_specs=pl.BlockSpec((rows, LANES), lambda i, be, lv: (i, 0)),
    )
    return pl.pallas_call(
        _expert_body,
        out_shape=jax.ShapeDtypeStruct((n_blocks * rows, LANES), F32),
        grid_spec=grid_spec,
        compiler_params=_params("arbitrary"),
        name="moe_experts",
    )(block_e, live, xs, wg.astype(BF16), wu.astype(BF16), wd.astype(BF16))


def _combine_body(n_tiles, tm, pstart_ref, cur_e, cur_r, nxt_e, nxt_r, ys_hbm, h_ref, gate_ref,
                  sg_ref, su_ref, sd_ref, g_ref, b_ref, o_ref, buf, sem):
    i = pl.program_id(0)
    slot = i % 2

    def gather(e_ref, r_ref, dst, dsem):
        def issue(tok, _):
            for k in range(TOP_K):
                src = pstart_ref[e_ref[0, 0, k * tm + tok]] + r_ref[0, 0, k * tm + tok]
                pltpu.make_async_copy(ys_hbm.at[_rows_of(src), :], dst.at[_rows_of(k * tm + tok), :], dsem).start()
            return 0
        lax.fori_loop(0, tm, issue, 0)

    @pl.when(i == 0)
    def _():
        gather(cur_e, cur_r, buf.at[0], sem.at[0])

    @pl.when(i + 1 < n_tiles)
    def _():
        gather(nxt_e, nxt_r, buf.at[1 - slot], sem.at[1 - slot])

    h = h_ref[...]
    hb = h.astype(BF16)
    hid = _silu(_dot(hb, sg_ref[...])) * _dot(hb, su_ref[...])
    acc = DN_ALPHA * h + _dot(hid.astype(BF16), sd_ref[...])
    pltpu.make_async_copy(ys_hbm.at[pl.ds(0, TOP_K * tm * TILE_ROWS), :], buf.at[slot], sem.at[slot]).wait()
    gates = gate_ref[...]
    cols = []
    for j in range(TILE_ROWS):
        part = None
        for k in range(TOP_K):
            rows = buf[slot, pl.ds(k * tm * TILE_ROWS + j, tm, stride=TILE_ROWS), :]
            term = gates[:, k:k + 1] * rows
            part = term if part is None else part + term
        cols.append(part)
    o_ref[...] = _layer_norm_rows(acc + jnp.concatenate(cols, axis=1), g_ref[...], b_ref[...])


def _combine_ln(h, ys, pstart, eidx_t, rank_t, gate_t, sh_gate, sh_up, sh_down, ln_g, ln_b, *, tm=256):
    t, d = h.shape
    de = sh_gate.shape[1]
    n_tiles = t // tm
    e3, r3 = _tile_major(eidx_t, tm), _tile_major(rank_t, tm)
    cur = pl.BlockSpec((1, 1, TOP_K * tm), lambda i, ps: (i, 0, 0), memory_space=pltpu.SMEM)
    nxt = pl.BlockSpec((1, 1, TOP_K * tm), lambda i, ps: (jnp.minimum(i + 1, n_tiles - 1), 0, 0),
                       memory_space=pltpu.SMEM)
    full = lambda shape: pl.BlockSpec(shape, lambda i, ps: (0, 0))
    grid_spec = pltpu.PrefetchScalarGridSpec(
        num_scalar_prefetch=1,
        grid=(n_tiles,),
        in_specs=[cur, cur, nxt, nxt,
                  pl.BlockSpec(memory_space=pl.ANY),
                  pl.BlockSpec((tm, d), lambda i, ps: (i, 0)),
                  pl.BlockSpec((tm, TOP_K), lambda i, ps: (i, 0)),
                  full((d, de)), full((d, de)), full((de, d)), full((1, d)), full((1, d))],
        out_specs=pl.BlockSpec((tm, d), lambda i, ps: (i, 0)),
        scratch_shapes=[pltpu.VMEM((2, TOP_K * tm * TILE_ROWS, LANES), F32), pltpu.SemaphoreType.DMA((2,))],
    )
    return pl.pallas_call(
        functools.partial(_combine_body, n_tiles, tm),
        out_shape=jax.ShapeDtypeStruct((t, d), F32),
        grid_spec=grid_spec,
        compiler_params=_params("arbitrary"),
        name="moe_combine_ln",
    )(pstart, e3, r3, e3, r3, ys, h, gate_t.T, sh_gate.astype(BF16), sh_up.astype(BF16), sh_down.astype(BF16),
      ln_g.reshape(1, d), ln_b.reshape(1, d))


def _moe_ln(h, h8, router_w, router_b, exp_gate, exp_up, exp_down, sh_gate, sh_up, sh_down, ln_g, ln_b):
    eidx_t, gate_t, rank_t, counts = _router(h, router_w, router_b)
    pstart, block_e, live, n_blocks = _block_plan(counts, eidx_t.size)
    xs = _dispatch(h8, pstart, live, eidx_t, rank_t, n_blocks)
    ys = _expert_blocks(xs, block_e, live, n_blocks, exp_gate, exp_up, exp_down)
    return _combine_ln(h, ys, pstart, eidx_t, rank_t, gate_t, sh_gate, sh_up, sh_down, ln_g, ln_b)


def _gdn_pre_body(ts, qkv_ref, w_ref, ba_ref, bat_ref, alog_r, dtb_r, alog_c, dtb_c,
                  q_out, k_out, v_out, beta_out, gcc_out, gcr_out, carry):
    i = pl.program_id(1)

    @pl.when(i == 0)
    def _():
        carry[...] = jnp.zeros_like(carry)

    sub8 = lax.broadcasted_iota(I32, (8, 1), 0)
    n_qk = C_KHEADS
    for hb in range(2 * C_KHEADS + C_VHEADS):
        sl = slice(hb * LANES, (hb + 1) * LANES)
        cur = qkv_ref[:, sl]
        tail = carry[:, sl]
        w = w_ref[:, sl]
        acc = cur * w[C_CONV - 1:C_CONV, :]
        for j in range(1, C_CONV):
            rolled = pltpu.roll(cur, j, 0)
            head = jnp.where(sub8 < j, pltpu.roll(tail, j, 0), rolled[0:8])
            shifted = jnp.concatenate([head, rolled[8:]], axis=0)
            acc = acc + shifted * w[C_CONV - 1 - j:C_CONV - j, :]
        carry[:, sl] = cur[ts - 8:ts, :]
        y = _silu(acc)
        if hb < 2 * n_qk:
            y = y * lax.rsqrt(jnp.sum(y * y, axis=1, keepdims=True) + 1e-6)
            if hb < n_qk:
                q_out[hb] = y * (C_DK ** -0.5)
            else:
                k_out[hb - n_qk] = y
        else:
            v_out[hb - 2 * n_qk] = y

    ba = ba_ref[...]
    beta_out[...] = _sigmoid(ba[:, 0:LANES])
    g_col = -jnp.exp(alog_r[...]) * _softplus(ba[:, LANES:2 * LANES] + dtb_r[...])
    rr = lax.broadcasted_iota(I32, (ts, ts), 0)
    cc = lax.broadcasted_iota(I32, (ts, ts), 1)
    same = (rr // C_CHUNK) == (cc // C_CHUNK)
    gcc_out[...] = _dot((same & (rr >= cc)).astype(F32), g_col, HIGHEST)
    g_row = -jnp.exp(alog_c[...]) * _softplus(bat_ref[C_VHEADS:2 * C_VHEADS, :] + dtb_c[...])
    gcr_out[...] = _dot(g_row, (same & (rr <= cc)).astype(F32), HIGHEST)


def _gdn_pre(p_main, ba, bat, conv_w, a_log, dt_bias, bn, s, *, ts=256):
    t = p_main.shape[0]
    nt = s // ts
    qkv_w = conv_w.shape[1]
    pad_r = lambda z: jnp.pad(z.astype(F32), (0, LANES - C_VHEADS)).reshape(1, LANES)
    col = lambda z: z.astype(F32).reshape(C_VHEADS, 1)
    full = lambda shape: pl.BlockSpec(shape, lambda b, i: (0,) * len(shape))
    hm = lambda nh: pl.BlockSpec((nh, ts, LANES), lambda b, i: (0, b * nt + i, 0))
    return pl.pallas_call(
        functools.partial(_gdn_pre_body, ts),
        out_shape=[jax.ShapeDtypeStruct((C_KHEADS, t, LANES), F32),
                   jax.ShapeDtypeStruct((C_KHEADS, t, LANES), F32),
                   jax.ShapeDtypeStruct((C_VHEADS, t, LANES), F32),
                   jax.ShapeDtypeStruct((t, LANES), F32),
                   jax.ShapeDtypeStruct((t, LANES), F32),
                   jax.ShapeDtypeStruct((C_VHEADS, t), F32)],
        grid=(bn, nt),
        in_specs=[pl.BlockSpec((ts, qkv_w), lambda b, i: (b * nt + i, 0)),
                  full((C_CONV, qkv_w)),
                  pl.BlockSpec((ts, 2 * LANES), lambda b, i: (b * nt + i, 0)),
                  pl.BlockSpec((2 * C_VHEADS, ts), lambda b, i: (0, b * nt + i)),
                  full((1, LANES)), full((1, LANES)), full((C_VHEADS, 1)), full((C_VHEADS, 1))],
        out_specs=[hm(C_KHEADS), hm(C_KHEADS), hm(C_VHEADS),
                   pl.BlockSpec((ts, LANES), lambda b, i: (b * nt + i, 0)),
                   pl.BlockSpec((ts, LANES), lambda b, i: (b * nt + i, 0)),
                   pl.BlockSpec((C_VHEADS, ts), lambda b, i: (0, b * nt + i))],
        scratch_shapes=[pltpu.VMEM((8, qkv_w), F32)],
        compiler_params=_params("parallel", "arbitrary"),
        name="gdn_pre",
    )(p_main, conv_w, ba, bat, pad_r(a_log), pad_r(dt_bias), col(a_log), col(dt_bias))


def _gdn_chunk_body(tb, gq, q_ref, k_ref, v_ref, z_ref, beta_ref, gcc_ref, gcr_ref, ng_ref, o_ref, state):
    c = C_CHUNK
    n = 2 * c
    hk0 = pl.program_id(1) * gq

    @pl.when(pl.program_id(2) == 0)
    def _():
        state[...] = jnp.zeros_like(state)

    row = lax.broadcasted_iota(I32, (n, n), 0)
    col = lax.broadcasted_iota(I32, (n, n), 1)
    same = (row // c) == (col // c)
    strict = same & (row > col)
    incl = same & (row >= col)
    lane = lax.broadcasted_iota(I32, (1, LANES), 1)
    top = lax.broadcasted_iota(I32, (n, 1), 0) < c
    ng = ng_ref[...]

    def pick(x, hv):
        return jnp.sum(jnp.where(lane == hv, x, 0.0), axis=1, keepdims=True)

    def one_head(ci, rs, hh):
        hk = hk0 + hh
        q, k = q_ref[hh, rs, :], k_ref[hh, rs, :]
        v = jnp.concatenate([v_ref[2 * hh, rs, :], v_ref[2 * hh + 1, rs, :]], axis=0)
        bt, gc = beta_ref[rs, :], gcc_ref[rs, :]
        beta = jnp.concatenate([pick(bt, 2 * hk), pick(bt, 2 * hk + 1)], axis=0)
        gcs = jnp.concatenate([pick(gc, 2 * hk), pick(gc, 2 * hk + 1)], axis=0)
        gcr = gcr_ref[hh, pl.ds(ci, 1), :]
        k2 = jnp.concatenate([k, k], axis=0)
        q2 = jnp.concatenate([q, q], axis=0)
        dec = jnp.exp(jnp.where(incl, gcs - gcr, -jnp.inf))
        kk = _bdot_nt(k2, k2)
        qk = _bdot_nt(q2, k2)
        a_blk = jnp.where(strict, kk * beta * dec, 0.0)
        attn = jnp.where(incl, qk * dec, 0.0)
        tinv = _tri_inv(a_blk, c)
        eg = jnp.exp(gcs)
        rhs = jnp.concatenate([v * beta, k2 * (beta * eg)], axis=1)
        sol = _bdot(tinv, rhs)
        u, w = sol[:, 0:LANES], sol[:, LANES:2 * LANES]
        qd = q2 * eg
        gl0, gl1 = gcs[c - 1:c, :], gcs[n - 1:n, :]
        kd = k2 * jnp.exp(jnp.where(top, gl0, gl1) - gcs)
        s0, s1 = state[2 * hh], state[2 * hh + 1]
        ws0 = _bdot(jnp.concatenate([w[0:c], qd[0:c]], axis=0), s0)
        ws1 = _bdot(jnp.concatenate([w[c:n], qd[c:n]], axis=0), s1)
        v_new = u - jnp.concatenate([ws0[0:c], ws1[0:c]], axis=0)
        o = jnp.concatenate([ws0[c:n], ws1[c:n]], axis=0) + _bdot(attn, v_new)
        state[2 * hh] = s0 * jnp.exp(gl0) + _bdot_tn(kd[0:c], v_new[0:c])
        state[2 * hh + 1] = s1 * jnp.exp(gl1) + _bdot_tn(kd[c:n], v_new[c:n])
        on = o * lax.rsqrt(jnp.mean(o * o, axis=1, keepdims=True) + C_NORM_EPS) * ng
        c0 = 2 * hh * LANES
        o_ref[rs, c0:c0 + LANES] = on[0:c] * _silu(z_ref[rs, c0:c0 + LANES])
        o_ref[rs, c0 + LANES:c0 + 2 * LANES] = on[c:n] * _silu(z_ref[rs, c0 + LANES:c0 + 2 * LANES])

    def chunk(ci, _):
        rs = pl.ds(pl.multiple_of(ci * c, c), c)
        for hh in range(gq):
            one_head(ci, rs, hh)
        return 0

    lax.fori_loop(0, tb // c, chunk, 0)


def _gdn_chunk(qh, kh, vh, p_main, beta_c, gc_c, gc_r, norm_g, bn, s, *, tb=512, gq=C_KHEADS):
    t = qh.shape[1]
    nt = s // tb
    z_off = (2 * C_KHEADS + C_VHEADS) // 2
    gcr = gc_r.reshape(C_KHEADS, 2, t // C_CHUNK, C_CHUNK).transpose(0, 2, 1, 3).reshape(
        C_KHEADS, t // C_CHUNK, 2 * C_CHUNK)
    tile = lambda f: pl.BlockSpec((gq, tb, LANES), f)
    return pl.pallas_call(
        functools.partial(_gdn_chunk_body, tb, gq),
        out_shape=jax.ShapeDtypeStruct((t, C_VHEADS * LANES), F32),
        grid=(bn, C_KHEADS // gq, nt),
        in_specs=[tile(lambda b, h, i: (h, b * nt + i, 0)),
                  tile(lambda b, h, i: (h, b * nt + i, 0)),
                  pl.BlockSpec((2 * gq, tb, LANES), lambda b, h, i: (h, b * nt + i, 0)),
                  pl.BlockSpec((tb, 2 * gq * LANES), lambda b, h, i: (b * nt + i, z_off // gq + h)),
                  pl.BlockSpec((tb, LANES), lambda b, h, i: (b * nt + i, 0)),
                  pl.BlockSpec((tb, LANES), lambda b, h, i: (b * nt + i, 0)),
                  pl.BlockSpec((gq, tb // C_CHUNK, 2 * C_CHUNK), lambda b, h, i: (h, b * nt + i, 0)),
                  pl.BlockSpec((1, LANES), lambda b, h, i: (0, 0))],
        out_specs=pl.BlockSpec((tb, 2 * gq * LANES), lambda b, h, i: (b * nt + i, h)),
        scratch_shapes=[pltpu.VMEM((2 * gq, C_DK, LANES), F32)],
        compiler_params=_params("parallel", "parallel", "arbitrary"),
        name="gdn_chunk",
    )(qh, kh, vh, p_main, beta_c, gc_c, gcr, norm_g.reshape(1, LANES))


def _attn_rwkv_layer(x2, bn, s, w_in, a_lambda, a_norm_g, b_mix, b_w0, b_w2, b_a0, b_a2, b_g2, b_k_k, b_k_a, b_r_k,
                     b_lnx_g, b_lnx_b, w_out, ln_g, ln_b, layer_idx):
    a_cols = 3 * A_HEADS * A_VDIM
    pa = _matmul(x2, w_in[:, :a_cols], tn=a_cols)
    pb = _matmul(x2, w_in[:, a_cols:], tn=(w_in.shape[1] - a_cols) // 2)
    ya = _diff_attention(pa, a_lambda, a_norm_g, bn, s, layer_idx)
    pre = _rwkv_pre(pb, b_mix, b_w0, b_w2, b_a0, b_a2, b_g2, b_k_k, b_k_a, b_r_k, bn, s)
    yb = _rwkv_scan(*pre, b_lnx_g, b_lnx_b, bn, s)
    wa = A_HEADS * A_VDIM
    return _matmul_res_ln([ya, yb], [w_out[:wa], w_out[wa:]], x2, ln_g, ln_b)


def _gdn_layer(x2, bn, s, w_in, conv_w, a_log, dt_bias, norm_g, w_out, ln_g, ln_b):
    qkv_w = conv_w.shape[1]
    main_w = qkv_w + C_VHEADS * LANES
    w_b = w_in[:, main_w:main_w + C_VHEADS]
    w_a = w_in[:, main_w + C_VHEADS:main_w + 2 * C_VHEADS]
    pad = lambda w: jnp.pad(w, ((0, 0), (0, LANES - C_VHEADS)))
    p_main = _matmul(x2, w_in[:, :main_w], tn=main_w // 4)
    ba = _matmul(x2, jnp.concatenate([pad(w_b), pad(w_a)], axis=1), tn=2 * LANES)
    bat = _matmul_t(jnp.concatenate([w_b, w_a], axis=1).T, x2)
    qh, kh, vh, beta_c, gc_c, gc_r = _gdn_pre(p_main, ba, bat, conv_w, a_log, dt_bias, bn, s)
    o = _gdn_chunk(qh, kh, vh, p_main, beta_c, gc_c, gc_r, norm_g, bn, s)
    return _matmul_res_ln([o], [w_out], x2, ln_g, ln_b)


def kernel(x, l0_w_in, l0_a_lambda, l0_a_norm_g, l0_b_mix, l0_b_w0, l0_b_w2, l0_b_a0, l0_b_a2, l0_b_g2, l0_b_k_k, l0_b_k_a, l0_b_r_k, l0_b_lnx_g, l0_b_lnx_b, l0_w_out, l0_ln1_g, l0_ln1_b, l0_router_w, l0_router_b, l0_exp_gate, l0_exp_up, l0_exp_down, l0_sh_gate, l0_sh_up, l0_sh_down, l0_ln2_g, l0_ln2_b, l1_w_in, l1_conv_w, l1_a_log, l1_dt_bias, l1_norm_g, l1_w_out, l1_ln1_g, l1_ln1_b, l1_router_w, l1_router_b, l1_exp_gate, l1_exp_up, l1_exp_down, l1_sh_gate, l1_sh_up, l1_sh_down, l1_ln2_g, l1_ln2_b):
    bn, s, d = x.shape
    x2 = x.reshape(bn * s, d)
    x2, x8 = _attn_rwkv_layer(x2, bn, s, l0_w_in, l0_a_lambda, l0_a_norm_g, l0_b_mix, l0_b_w0, l0_b_w2, l0_b_a0,
                              l0_b_a2, l0_b_g2, l0_b_k_k, l0_b_k_a, l0_b_r_k, l0_b_lnx_g, l0_b_lnx_b, l0_w_out,
                              l0_ln1_g, l0_ln1_b, 0)
    x2 = _moe_ln(x2, x8, l0_router_w, l0_router_b, l0_exp_gate, l0_exp_up, l0_exp_down, l0_sh_gate, l0_sh_up,
                 l0_sh_down, l0_ln2_g, l0_ln2_b)
    x2, x8 = _gdn_layer(x2, bn, s, l1_w_in, l1_conv_w, l1_a_log, l1_dt_bias, l1_norm_g, l1_w_out, l1_ln1_g,
                        l1_ln1_b)
    x2 = _moe_ln(x2, x8, l1_router_w, l1_router_b, l1_exp_gate, l1_exp_up, l1_exp_down, l1_sh_gate, l1_sh_up,
                 l1_sh_down, l1_ln2_g, l1_ln2_b)
    return x2.reshape(bn, s, d)
```

```python
import functools
import math

import jax
import jax.numpy as jnp
import numpy as np
from jax import lax
from jax.experimental import pallas as pl
from jax.experimental.pallas import tpu as pltpu

F32 = jnp.float32
BF16 = jnp.bfloat16
I32 = jnp.int32
HIGHEST = lax.Precision.HIGHEST

LANES = 128
TILE_ROWS = 8
VMEM_LIMIT = 56 * 1024 * 1024

DEPTH = 2
DN_ALPHA = (2.0 * DEPTH) ** 0.25
LN_EPS = 1e-5

A_HEADS = 4
A_VDIM = 128
A_QKDIM = 64
A_NORM_EPS = 1e-5

B_HEAD = 64
B_WIDTH = 512
B_LNX_EPS = 64e-5
B_CHUNK = 64

C_DK = 128
C_KHEADS = 8
C_VHEADS = 16
C_CONV = 4
C_CHUNK = 64
C_NORM_EPS = 1e-6

N_EXPERTS = 64
TOP_K = 8
N_GROUPS = 8
TOPK_GROUPS = 4
ROUTE_SCALE = 2.5
MOE_ROWS = 256


def _params(*sem):
    return pltpu.CompilerParams(dimension_semantics=sem, vmem_limit_bytes=VMEM_LIMIT)


def _dot(a, b, precision=None):
    return jnp.dot(a, b, preferred_element_type=F32, precision=precision)


def _dot_nt(a, b, precision=None):
    return lax.dot_general(a, b, (((1,), (1,)), ((), ())), preferred_element_type=F32, precision=precision)


def _dot_tn(a, b, precision=None):
    return lax.dot_general(a, b, (((0,), (0,)), ((), ())), preferred_element_type=F32, precision=precision)


def _split(x):
    hi = x.astype(BF16)
    return hi, (x - hi.astype(F32)).astype(BF16)


def _dot_sel(sel, x):
    hi, lo = _split(x)
    return _dot(sel, hi) + _dot(sel, lo)


def _dot_by_sel(x, sel):
    hi, lo = _split(x)
    return _dot(hi, sel) + _dot(lo, sel)


def _dot3(a, b):
    ah, al = _split(a)
    bh, bl = _split(b)
    return _dot(ah, bh) + (_dot(ah, bl) + _dot(al, bh))


def _bdot(a, b):
    return _dot(a.astype(BF16), b.astype(BF16))


def _bdot_nt(a, b):
    return _dot_nt(a.astype(BF16), b.astype(BF16))


def _bdot_tn(a, b):
    return _dot_tn(a.astype(BF16), b.astype(BF16))


def _sigmoid(x):
    return 1.0 / (1.0 + jnp.exp(-x))


def _silu(x):
    return x * _sigmoid(x)


def _softplus(x):
    return jnp.maximum(x, 0.0) + jnp.log(1.0 + jnp.exp(-jnp.abs(x)))


def _mm_body(x_ref, w_ref, o_ref):
    o_ref[...] = _dot(x_ref[...].astype(BF16), w_ref[...]).astype(o_ref.dtype)


def _matmul(x, w, *, tm=512, tn=512, out_dtype=F32):
    m, k = x.shape
    n = w.shape[1]
    tn = min(tn, n)
    assert m % tm == 0 and n % tn == 0
    return pl.pallas_call(
        _mm_body,
        out_shape=jax.ShapeDtypeStruct((m, n), out_dtype),
        grid=(m // tm, n // tn),
        in_specs=[pl.BlockSpec((tm, k), lambda i, j: (i, 0)),
                  pl.BlockSpec((k, tn), lambda i, j: (0, j))],
        out_specs=pl.BlockSpec((tm, tn), lambda i, j: (i, j)),
        compiler_params=_params("parallel", "arbitrary"),
        name="matmul",
    )(x, w.astype(BF16))


def _mm_nt_body(w_ref, x_ref, o_ref):
    o_ref[...] = _dot_nt(w_ref[...], x_ref[...].astype(BF16))


def _matmul_t(wt, x, *, tm=512):
    n, k = wt.shape
    m = x.shape[0]
    return pl.pallas_call(
        _mm_nt_body,
        out_shape=jax.ShapeDtypeStruct((n, m), F32),
        grid=(m // tm,),
        in_specs=[pl.BlockSpec((n, k), lambda i: (0, 0)),
                  pl.BlockSpec((tm, k), lambda i: (i, 0))],
        out_specs=pl.BlockSpec((n, tm), lambda i: (0, i)),
        compiler_params=_params("parallel"),
        name="matmul_t",
    )(wt.astype(BF16), x)


def _layer_norm_rows(y, g, b):
    mu = jnp.mean(y, axis=-1, keepdims=True)
    d = y - mu
    var = jnp.mean(d * d, axis=-1, keepdims=True)
    return d * lax.rsqrt(var + LN_EPS) * g + b


def _mm_ln_body(n_parts, tm, *refs):
    parts = refs[:n_parts]
    ws = refs[n_parts:2 * n_parts]
    res_ref, g_ref, b_ref, o_ref, o8_ref = refs[2 * n_parts:]
    acc = DN_ALPHA * res_ref[...]
    for p_ref, w_ref in zip(parts, ws):
        acc = acc + _dot(p_ref[...].astype(BF16), w_ref[...])
    y = _layer_norm_rows(acc, g_ref[...], b_ref[...])
    o_ref[...] = y
    _to_tile_rows(o8_ref, y, tm)


def _matmul_res_ln(parts, ws, res, g, b, *, tm=256):
    m, d = res.shape
    assert d == TILE_ROWS * LANES
    n_parts = len(parts)
    in_specs = [pl.BlockSpec((tm, p.shape[1]), lambda i: (i, 0)) for p in parts]
    in_specs += [pl.BlockSpec(w.shape, lambda i: (0, 0)) for w in ws]
    in_specs += [pl.BlockSpec((tm, d), lambda i: (i, 0)),
                 pl.BlockSpec((1, d), lambda i: (0, 0)),
                 pl.BlockSpec((1, d), lambda i: (0, 0))]
    return pl.pallas_call(
        functools.partial(_mm_ln_body, n_parts, tm),
        out_shape=[jax.ShapeDtypeStruct((m, d), F32), jax.ShapeDtypeStruct((m * TILE_ROWS, LANES), F32)],
        grid=(m // tm,),
        in_specs=in_specs,
        out_specs=[pl.BlockSpec((tm, d), lambda i: (i, 0)), pl.BlockSpec((tm * TILE_ROWS, LANES), lambda i: (i, 0))],
        compiler_params=_params("parallel"),
        name="matmul_res_ln",
    )(*parts, *[w.astype(BF16) for w in ws], res, g.reshape(1, d), b.reshape(1, d))


def _attn_body(seq_blocks, tq, lam_init, slopes_ref, lam_ref, g_ref, q_ref, k_ref, v_ref, o_ref):
    h = pl.program_id(1)
    qi = pl.program_id(2)
    slope = slopes_ref[h]
    lp = lam_ref[...]
    lam = (jnp.exp(jnp.sum(lp[0:1] * lp[1:2], axis=1, keepdims=True))
           - jnp.exp(jnp.sum(lp[2:3] * lp[3:4], axis=1, keepdims=True)) + lam_init)
    lane = lax.broadcasted_iota(I32, (1, A_VDIM), 1)
    q = q_ref[...] * (A_QKDIM ** -0.5)
    q0 = jnp.where(lane < A_QKDIM, q, 0.0).astype(BF16)
    q1 = jnp.where(lane >= A_QKDIM, q, 0.0).astype(BF16)
    rel = (lax.broadcasted_iota(I32, (tq, tq), 0) - lax.broadcasted_iota(I32, (tq, tq), 1))

    def step(j, carry):
        m0, l0, a0, m1, l1, a1 = carry
        kb = k_ref[pl.ds(pl.multiple_of(j * tq, tq), tq), :].astype(BF16)
        vb = v_ref[pl.ds(pl.multiple_of(j * tq, tq), tq), :].astype(BF16)
        dist = rel + (qi - j) * tq
        bias = slope * dist.astype(F32)
        keep = dist >= 0

        def one(qc, m, l, a):
            s = jnp.where(keep, _dot_nt(qc, kb) - bias, -jnp.inf)
            m_new = jnp.maximum(m, jnp.max(s, axis=1, keepdims=True))
            alpha = jnp.exp(m - m_new)
            p = jnp.exp(s - m_new)
            l_new = alpha * l + jnp.sum(p, axis=1, keepdims=True)
            a_new = alpha * a + _dot(p.astype(BF16), vb)
            return m_new, l_new, a_new

        m0, l0, a0 = one(q0, m0, l0, a0)
        m1, l1, a1 = one(q1, m1, l1, a1)
        return m0, l0, a0, m1, l1, a1

    neg = jnp.full((tq, 1), -jnp.inf, F32)
    zero1 = jnp.zeros((tq, 1), F32)
    zero = jnp.zeros((tq, A_VDIM), F32)
    _, l0, a0, _, l1, a1 = lax.fori_loop(0, qi + 1, step, (neg, zero1, zero, neg, zero1, zero))
    o = a0 / l0 - lam * (a1 / l1)
    ms = jnp.mean(o * o, axis=1, keepdims=True)
    o_ref[...] = o * lax.rsqrt(ms + A_NORM_EPS) * g_ref[...] * (1.0 - lam_init)


def _diff_attention(pa, lam_params, norm_g, bn, s, layer_idx, *, tq=256):
    t = pa.shape[0]
    nq = s // tq
    lam_init = 0.8 - 0.6 * math.exp(-0.3 * layer_idx)
    slopes = jnp.asarray(np.array([2.0 ** (-8.0 * (i + 1) / A_HEADS) for i in range(A_HEADS)], dtype=np.float32))
    body = functools.partial(_attn_body, nq, tq, lam_init)
    return pl.pallas_call(
        body,
        out_shape=jax.ShapeDtypeStruct((t, A_HEADS * A_VDIM), F32),
        grid=(bn, A_HEADS, nq),
        in_specs=[pl.BlockSpec(memory_space=pltpu.SMEM),
                  pl.BlockSpec((4, A_QKDIM), lambda b, h, i: (0, 0)),
                  pl.BlockSpec((1, A_VDIM), lambda b, h, i: (0, 0)),
                  pl.BlockSpec((tq, A_VDIM), lambda b, h, i: (b * nq + i, h)),
                  pl.BlockSpec((s, A_VDIM), lambda b, h, i: (b, A_HEADS + h)),
                  pl.BlockSpec((s, A_VDIM), lambda b, h, i: (b, 2 * A_HEADS + h))],
        out_specs=pl.BlockSpec((tq, A_VDIM), lambda b, h, i: (b * nq + i, h)),
        compiler_params=_params("parallel", "parallel", "arbitrary"),
        name="diff_attention",
    )(slopes, lam_params.astype(F32), norm_g.reshape(1, A_VDIM), pa, pa, pa)


def _tri_inv(a, blk):
    n = a.shape[0]
    row = lax.broadcasted_iota(I32, (n, n), 0)
    col = lax.broadcasted_iota(I32, (n, n), 1)
    eye = (row == col).astype(F32)

    def same(m):
        return (row // m) == (col // m)

    d = jnp.where(same(8), a, 0.0)
    d2 = _bdot(d, d)
    d4 = _bdot(d2, d2)
    t = _bdot(_bdot(eye - d, eye + d2), eye + d4)
    m = 8
    while m < blk:
        e = jnp.where(same(2 * m) & jnp.logical_not(same(m)), a, 0.0)
        t = t - _bdot(t, _bdot(e, t))
        m *= 2
    return t


def _group_sum(x, ones_blk):
    return _dot_by_sel(x, ones_blk)


def _rwkv_pre_body(ts, pb_ref, mix_ref, w0_ref, w2_ref, a0_ref, a2_ref, g2_ref, kk_ref, ka_ref, rk_ref, ones_ref,
                   r_out, k_out, v_out, kap_out, b_out, ld_out, g_out, bv_out, carry):
    i = pl.program_id(1)

    @pl.when(i == 0)
    def _():
        carry[...] = jnp.zeros_like(carry)

    cur = pb_ref[...]
    rolled = pltpu.roll(cur, 1, 0)
    first = lax.broadcasted_iota(I32, (ts, 1), 0) == 0
    prev = jnp.where(first, carry[0:1, :], rolled)
    carry[0:1, :] = cur[ts - 1:ts, :]
    x = cur + (prev - cur) * mix_ref[...]
    r = x[:, 0:B_WIDTH]
    k = x[:, B_WIDTH:2 * B_WIDTH]
    v = x[:, 2 * B_WIDTH:3 * B_WIDTH]
    xwa = x[:, 3 * B_WIDTH:3 * B_WIDTH + LANES]
    xg = x[:, 3 * B_WIDTH + LANES:3 * B_WIDTH + 2 * LANES]
    w_log = -_softplus(-(w0_ref[...] + _dot(jnp.tanh(xwa).astype(BF16), w2_ref[...]))) - 0.5
    a = _sigmoid(a0_ref[...] + _dot(xwa.astype(BF16), a2_ref[...]))
    g = _dot(_sigmoid(xg).astype(BF16), g2_ref[...])
    ones_blk = ones_ref[...]
    kkr = k * kk_ref[...]
    kp = k * (1.0 + (a - 1.0) * ka_ref[...])
    rkr = r * kp * rk_ref[...]
    for c in range(B_WIDTH // LANES):
        sl = slice(c * LANES, (c + 1) * LANES)
        kn = kkr[:, sl]
        kap = kn * lax.rsqrt(_group_sum(kn * kn, ones_blk) + 1e-6)
        kap_out[:, sl] = kap
        b_out[:, sl] = kap * a[:, sl]
        bv_out[:, sl] = _group_sum(rkr[:, sl], ones_blk) * v[:, sl]
    r_out[...] = r
    k_out[...] = kp
    v_out[...] = v
    ld_out[...] = -jnp.exp(w_log)
    g_out[...] = g


def _head_group_ones():
    idx = np.arange(LANES) // B_HEAD
    return jnp.asarray((idx[:, None] == idx[None, :]).astype(np.float32)).astype(BF16)


def _rwkv_pre(pb, mix, w0, w2, a0, a2, g2, k_k, k_a, r_k, bn, s, *, ts=256):
    t, cols = pb.shape
    nt = s // ts
    lora = w2.shape[0]
    w2p = jnp.concatenate([w2, jnp.zeros_like(w2)], axis=0).astype(BF16)
    a2p = jnp.concatenate([jnp.zeros_like(a2), a2], axis=0).astype(BF16)
    assert lora * 2 == LANES
    row = lambda z: z.reshape(1, -1)
    full = lambda shape: pl.BlockSpec(shape, lambda b, i: (0, 0))
    out = jax.ShapeDtypeStruct((t, B_WIDTH), F32)
    tile = pl.BlockSpec((ts, B_WIDTH), lambda b, i: (b * nt + i, 0))
    return pl.pallas_call(
        functools.partial(_rwkv_pre_body, ts),
        out_shape=[out] * 8,
        grid=(bn, nt),
        in_specs=[pl.BlockSpec((ts, cols), lambda b, i: (b * nt + i, 0)),
                  full((1, cols)), full((1, B_WIDTH)), full((LANES, B_WIDTH)), full((1, B_WIDTH)),
                  full((LANES, B_WIDTH)), full((LANES, B_WIDTH)), full((1, B_WIDTH)), full((1, B_WIDTH)),
                  full((1, B_WIDTH)), full((LANES, LANES))],
        out_specs=[tile] * 8,
        scratch_shapes=[pltpu.VMEM((8, cols), F32)],
        compiler_params=_params("parallel", "arbitrary"),
        name="rwkv_pre",
    )(pb, row(mix), row(w0), w2p, row(a0), a2p, g2.astype(BF16), row(k_k), row(k_a), row(r_k), _head_group_ones())


def _rwkv_scan_body(c, nb, r_ref, k_ref, v_ref, kap_ref, b_ref, ld_ref, g_ref, bv_ref, lng_ref, lnb_ref, ones_ref,
                    o_ref, state):
    @pl.when(pl.program_id(1) == 0)
    def _():
        state[...] = jnp.zeros_like(state)

    n = 2 * c
    row = lax.broadcasted_iota(I32, (n, n), 0)
    col = lax.broadcasted_iota(I32, (n, n), 1)
    same = (row // c) == (col // c)
    strict = same & (row > col)
    incl = same & (row >= col)
    trow = lax.broadcasted_iota(I32, (c, c), 0)
    tcol = lax.broadcasted_iota(I32, (c, c), 1)
    tril = (trow >= tcol).astype(BF16)
    head0 = lax.broadcasted_iota(I32, (1, LANES), 1) < B_HEAD
    ones_blk = ones_ref[...]

    def sel(z):
        return jnp.where(head0, z[0:c], z[c:n])

    def dup(z):
        return jnp.concatenate([z, z], axis=0)

    n_pairs = B_WIDTH // LANES
    for bb, p in [(bb, p) for bb in range(nb) for p in range(n_pairs)]:
        sl = slice(p * LANES, (p + 1) * LANES)
        ld = ld_ref[bb, :, sl]
        r, k, v = r_ref[bb, :, sl], k_ref[bb, :, sl], v_ref[bb, :, sl]
        kap, b = kap_ref[bb, :, sl], b_ref[bb, :, sl]
        cum = _dot_sel(tril, ld)
        gam = jnp.exp(cum)
        einv = jnp.exp(-cum)
        kt = kap * jnp.exp(cum - ld)
        rt = r * gam
        kh = k * einv
        bh = b * einv
        g_last = gam[c - 1:c, :]
        x = jnp.concatenate([jnp.where(head0, kt, 0.0), jnp.where(head0, 0.0, kt),
                             jnp.where(head0, rt, 0.0), jnp.where(head0, 0.0, rt)], axis=0)
        y = jnp.concatenate([kh, kh, bh, bh], axis=0)
        m2 = _bdot_nt(x, y)
        a_kk = jnp.where(strict, m2[0:n, 0:n], 0.0)
        a_bk = jnp.where(strict, m2[0:n, n:2 * n], 0.0)
        b_rk = jnp.where(incl, m2[n:2 * n, 0:n], 0.0)
        b_rb = jnp.where(incl, m2[n:2 * n, n:2 * n], 0.0)
        tinv = _tri_inv(a_bk, c)
        s_old = state[bb * n_pairs + p]
        ps = _bdot_nt(jnp.concatenate([kt, rt], axis=0), s_old)
        vv = dup(v)
        rhs = ps[0:c] + sel(_bdot(a_kk, vv))
        u = sel(_bdot(tinv, dup(rhs)))
        yc = ps[c:n] + sel(_bdot(b_rk, vv) - _bdot(b_rb, dup(u)))
        upd = _bdot_tn(jnp.concatenate([v, u], axis=0),
                       jnp.concatenate([kh * g_last, -(bh * g_last)], axis=0))
        state[bb * n_pairs + p] = s_old * g_last + jnp.where(same, upd, 0.0)
        mu = _group_sum(yc, ones_blk) * (1.0 / B_HEAD)
        dy = yc - mu
        var = _group_sum(dy * dy, ones_blk) * (1.0 / B_HEAD)
        yn = dy * lax.rsqrt(var + B_LNX_EPS) * lng_ref[:, sl] + lnb_ref[:, sl]
        o_ref[bb, :, sl] = (yn + bv_ref[bb, :, sl]) * g_ref[bb, :, sl]


def _rwkv_scan(r, k, v, kap, b, ld, g, bv, lnx_g, lnx_b, bn, s, *, c=B_CHUNK, nb=2):
    t = r.shape[0]
    nc = s // c
    nb = math.gcd(nb, bn)
    seq = lambda z: z.reshape(bn, s, B_WIDTH)
    tile = pl.BlockSpec((nb, c, B_WIDTH), lambda bi, i: (bi, i, 0))
    vec = pl.BlockSpec((1, B_WIDTH), lambda bi, i: (0, 0))
    out = pl.pallas_call(
        functools.partial(_rwkv_scan_body, c, nb),
        out_shape=jax.ShapeDtypeStruct((bn, s, B_WIDTH), F32),
        grid=(bn // nb, nc),
        in_specs=[tile] * 8 + [vec, vec, pl.BlockSpec((LANES, LANES), lambda bi, i: (0, 0))],
        out_specs=tile,
        scratch_shapes=[pltpu.VMEM((nb * (B_WIDTH // LANES), LANES, LANES), F32)],
        compiler_params=_params("parallel", "arbitrary"),
        name="rwkv_scan",
    )(seq(r), seq(k), seq(v), seq(kap), seq(b), seq(ld), seq(g), seq(bv),
      lnx_g.reshape(1, -1), lnx_b.reshape(1, -1), _head_group_ones())
    return out.reshape(t, B_WIDTH)


def _first_index_of_max(vals, idx, sentinel):
    m = jnp.max(vals, axis=0, keepdims=True)
    first = jnp.min(jnp.where(vals == m, idx, sentinel), axis=0, keepdims=True)
    return m, first


def _router_body(tm, w_ref, bias_ref, h_ref, eidx_ref, gate_ref, rank_ref, count_ref, carry):
    @pl.when(pl.program_id(0) == 0)
    def _():
        carry[...] = jnp.zeros_like(carry)

    per_group = N_EXPERTS // N_GROUPS
    logits = _dot_nt(w_ref[...], h_ref[...], HIGHEST)
    scores = _sigmoid(logits)
    choice = scores + bias_ref[...]
    sub = lax.broadcasted_iota(I32, (per_group, tm), 0)
    grp_rows = []
    for g in range(N_GROUPS):
        cg = choice[g * per_group:(g + 1) * per_group, :]
        m1, first = _first_index_of_max(cg, sub, per_group)
        m2 = jnp.max(jnp.where(sub == first, -jnp.inf, cg), axis=0, keepdims=True)
        grp_rows.append(m1 + m2)
    gs = jnp.concatenate(grp_rows, axis=0)
    gidx = lax.broadcasted_iota(I32, (N_GROUPS, tm), 0)
    gsel = jnp.zeros((N_GROUPS, tm), jnp.bool_)
    for _ in range(TOPK_GROUPS):
        _, first = _first_index_of_max(gs, gidx, N_GROUPS)
        pick = gidx == first
        gsel = gsel | pick
        gs = jnp.where(pick, -jnp.inf, gs)
    ch = jnp.concatenate(
        [jnp.where(gsel[g:g + 1, :], choice[g * per_group:(g + 1) * per_group, :], -jnp.inf)
         for g in range(N_GROUPS)], axis=0)
    eidx = lax.broadcasted_iota(I32, (N_EXPERTS, tm), 0)
    idx_rows, score_rows, picks = [], [], []
    for _ in range(TOP_K):
        _, first = _first_index_of_max(ch, eidx, N_EXPERTS)
        pick = eidx == first
        picks.append(pick)
        idx_rows.append(first)
        score_rows.append(jnp.sum(jnp.where(pick, scores, 0.0), axis=0, keepdims=True))
        ch = jnp.where(pick, -jnp.inf, ch)
    sc = jnp.concatenate(score_rows, axis=0)
    eidx_ref[...] = jnp.concatenate(idx_rows, axis=0)
    gate_ref[...] = sc / jnp.sum(sc, axis=0, keepdims=True) * ROUTE_SCALE

    onehot = picks[0].astype(F32)
    for pick in picks[1:]:
        onehot = onehot + pick.astype(F32)
    before = (lax.broadcasted_iota(I32, (tm, tm), 0) < lax.broadcasted_iota(I32, (tm, tm), 1)).astype(BF16)
    seen = carry[...] + _dot(onehot.astype(BF16), before)
    rank_ref[...] = jnp.concatenate(
        [jnp.sum(jnp.where(pick, seen, 0.0), axis=0, keepdims=True) for pick in picks], axis=0).astype(I32)
    total = carry[...] + jnp.sum(onehot, axis=1, keepdims=True)
    carry[...] = total
    count_ref[...] = total.astype(I32)


def _router(h, router_w, router_b, *, tm=512):
    t, d = h.shape
    row_blk = pl.BlockSpec((TOP_K, tm), lambda i: (0, i))
    return pl.pallas_call(
        functools.partial(_router_body, tm),
        out_shape=[jax.ShapeDtypeStruct((TOP_K, t), I32), jax.ShapeDtypeStruct((TOP_K, t), F32),
                   jax.ShapeDtypeStruct((TOP_K, t), I32), jax.ShapeDtypeStruct((N_EXPERTS, 1), I32)],
        grid=(t // tm,),
        in_specs=[pl.BlockSpec((N_EXPERTS, d), lambda i: (0, 0)),
                  pl.BlockSpec((N_EXPERTS, 1), lambda i: (0, 0)),
                  pl.BlockSpec((tm, d), lambda i: (i, 0))],
        out_specs=[row_blk, row_blk, row_blk, pl.BlockSpec((N_EXPERTS, 1), lambda i: (0, 0))],
        scratch_shapes=[pltpu.VMEM((N_EXPERTS, 1), F32)],
        compiler_params=_params("arbitrary"),
        name="moe_router",
    )(router_w.T, router_b.reshape(N_EXPERTS, 1), h)


def _block_plan(counts, n_assign):
    counts = counts.reshape(N_EXPERTS)
    padded = (counts + MOE_ROWS - 1) // MOE_ROWS * MOE_ROWS
    pend = jnp.cumsum(padded)
    pstart = (pend - padded).astype(I32)
    n_blocks = (n_assign + N_EXPERTS * (MOE_ROWS - 1) + MOE_ROWS - 1) // MOE_ROWS
    starts = jnp.arange(n_blocks, dtype=I32) * MOE_ROWS
    block_e = jnp.minimum(jnp.sum(starts[:, None] >= pend[None, :], axis=1), N_EXPERTS - 1).astype(I32)
    live = jnp.clip(counts[block_e] - (starts - pstart[block_e]), 0, MOE_ROWS).astype(I32)
    return pstart, block_e, live, n_blocks


def _tile_major(idx_t, tm):
    k, t = idx_t.shape
    return idx_t.reshape(k, t // tm, tm).transpose(1, 0, 2).reshape(t // tm, 1, k * tm)


def _rows_of(tok):
    return pl.ds(pl.multiple_of(tok * TILE_ROWS, TILE_ROWS), TILE_ROWS)


def _to_tile_rows(ref, y, n):
    for j in range(TILE_ROWS):
        ref[pl.ds(j, n, stride=TILE_ROWS), :] = y[:, j * LANES:(j + 1) * LANES]


def _dispatch_body(tm, n_blocks, pstart_ref, live_ref, eidx_ref, rank_ref, h8_ref, xs_hbm, zeros, sem, zsem):
    @pl.when(pl.program_id(0) == 0)
    def _():
        zeros[...] = jnp.zeros_like(zeros)
        rows = MOE_ROWS * TILE_ROWS

        def fill(wait):
            def body(b, _):
                @pl.when(live_ref[b] < MOE_ROWS)
                def _():
                    cp = pltpu.make_async_copy(zeros, xs_hbm.at[pl.ds(pl.multiple_of(b * rows, rows), rows), :],
                                               zsem)
                    cp.wait() if wait else cp.start()
                return 0
            return body

        lax.fori_loop(0, n_blocks, fill(False), 0)
        lax.fori_loop(0, n_blocks, fill(True), 0)

    def issue(tok, _):
        src = h8_ref.at[_rows_of(tok), :]
        for k in range(TOP_K):
            slot = pstart_ref[eidx_ref[0, 0, k * tm + tok]] + rank_ref[0, 0, k * tm + tok]
            pltpu.make_async_copy(src, xs_hbm.at[_rows_of(slot), :], sem).start(priority=k % 2)
        return 0

    lax.fori_loop(0, tm, issue, 0)
    for _ in range(TOP_K):
        pltpu.make_async_copy(h8_ref, xs_hbm.at[pl.ds(0, tm * TILE_ROWS), :], sem).wait()


def _dispatch(h8, pstart, live, eidx_t, rank_t, n_blocks, *, tm=256):
    t = eidx_t.shape[1]
    grid_spec = pltpu.PrefetchScalarGridSpec(
        num_scalar_prefetch=2,
        grid=(t // tm,),
        in_specs=[pl.BlockSpec((1, 1, TOP_K * tm), lambda i, ps, lv: (i, 0, 0), memory_space=pltpu.SMEM),
                  pl.BlockSpec((1, 1, TOP_K * tm), lambda i, ps, lv: (i, 0, 0), memory_space=pltpu.SMEM),
                  pl.BlockSpec((tm * TILE_ROWS, LANES), lambda i, ps, lv: (i, 0))],
        out_specs=pl.BlockSpec(memory_space=pl.ANY),
        scratch_shapes=[pltpu.VMEM((MOE_ROWS * TILE_ROWS, LANES), F32),
                        pltpu.SemaphoreType.DMA(()), pltpu.SemaphoreType.DMA(())],
    )
    return pl.pallas_call(
        functools.partial(_dispatch_body, tm, n_blocks),
        out_shape=jax.ShapeDtypeStruct((n_blocks * MOE_ROWS * TILE_ROWS, LANES), F32),
        grid_spec=grid_spec,
        compiler_params=_params("arbitrary"),
        name="moe_dispatch",
    )(pstart, live, _tile_major(eidx_t, tm), _tile_major(rank_t, tm), h8)


def _expert_body(be_ref, live_ref, x_ref, wg_ref, wu_ref, wd_ref, o_ref):
    n_live = live_ref[pl.program_id(0)]

    @pl.when(n_live > 0)
    def _():
        x = jnp.concatenate([x_ref[pl.ds(j, MOE_ROWS, stride=TILE_ROWS), :] for j in range(TILE_ROWS)], axis=1)
        x = x.astype(BF16)
        hid = _silu(_dot(x, wg_ref[...])) * _dot(x, wu_ref[...])
        _to_tile_rows(o_ref, _dot(hid.astype(BF16), wd_ref[...]), MOE_ROWS)

    @pl.when(n_live == 0)
    def _():
        o_ref[...] = jnp.zeros_like(o_ref)


def _expert_blocks(xs, block_e, live, n_blocks, wg, wu, wd):
    d, de = wg.shape[1], wg.shape[2]
    rows = MOE_ROWS * TILE_ROWS
    grid_spec = pltpu.PrefetchScalarGridSpec(
        num_scalar_prefetch=2,
        grid=(n_blocks,),
        in_specs=[pl.BlockSpec((rows, LANES), lambda i, be, lv: (i, 0)),
                  pl.BlockSpec((None, d, de), lambda i, be, lv: (be[i], 0, 0)),
                  pl.BlockSpec((None, d, de), lambda i, be, lv: (be[i], 0, 0)),
                  pl.BlockSpec((None, de, d), lambda i, be, lv: (be[i], 0, 0))],
        out_specs=pl.BlockSpec((rows, LANES), lambda i, be, lv: (i, 0)),
    )
    return pl.pallas_call(
        _expert_body,
        out_shape=jax.ShapeDtypeStruct((n_blocks * rows, LANES), F32),
        grid_spec=grid_spec,
        compiler_params=_params("arbitrary"),
        name="moe_experts",
    )(block_e, live, xs, wg.astype(BF16), wu.astype(BF16), wd.astype(BF16))


def _combine_body(n_tiles, tm, pstart_ref, cur_e, cur_r, nxt_e, nxt_r, ys_hbm, h_ref, gate_ref,
                  sg_ref, su_ref, sd_ref, g_ref, b_ref, o_ref, buf, sem):
    i = pl.program_id(0)
    slot = i % 2

    def gather(e_ref, r_ref, dst, dsem):
        def issue(tok, _):
            for k in range(TOP_K):
                src = pstart_ref[e_ref[0, 0, k * tm + tok]] + r_ref[0, 0, k * tm + tok]
                pltpu.make_async_copy(ys_hbm.at[_rows_of(src), :], dst.at[_rows_of(k * tm + tok), :],
                                      dsem).start(priority=k % 2)
            return 0
        lax.fori_loop(0, tm, issue, 0)

    @pl.when(i == 0)
    def _():
        gather(cur_e, cur_r, buf.at[0], sem.at[0])

    @pl.when(i + 1 < n_tiles)
    def _():
        gather(nxt_e, nxt_r, buf.at[1 - slot], sem.at[1 - slot])

    h = h_ref[...]
    hb = h.astype(BF16)
    hid = _silu(_dot(hb, sg_ref[...])) * _dot(hb, su_ref[...])
    acc = DN_ALPHA * h + _dot(hid.astype(BF16), sd_ref[...])
    pltpu.make_async_copy(ys_hbm.at[pl.ds(0, TOP_K * tm * TILE_ROWS), :], buf.at[slot], sem.at[slot]).wait()
    gates = gate_ref[...]
    cols = []
    for j in range(TILE_ROWS):
        part = None
        for k in range(TOP_K):
            rows = buf[slot, pl.ds(k * tm * TILE_ROWS + j, tm, stride=TILE_ROWS), :]
            term = gates[:, k:k + 1] * rows
            part = term if part is None else part + term
        cols.append(part)
    o_ref[...] = _layer_norm_rows(acc + jnp.concatenate(cols, axis=1), g_ref[...], b_ref[...])


def _combine_ln(h, ys, pstart, eidx_t, rank_t, gate_t, sh_gate, sh_up, sh_down, ln_g, ln_b, *, tm=256):
    t, d = h.shape
    de = sh_gate.shape[1]
    n_tiles = t // tm
    e3, r3 = _tile_major(eidx_t, tm), _tile_major(rank_t, tm)
    cur = pl.BlockSpec((1, 1, TOP_K * tm), lambda i, ps: (i, 0, 0), memory_space=pltpu.SMEM)
    nxt = pl.BlockSpec((1, 1, TOP_K * tm), lambda i, ps: (jnp.minimum(i + 1, n_tiles - 1), 0, 0),
                       memory_space=pltpu.SMEM)
    full = lambda shape: pl.BlockSpec(shape, lambda i, ps: (0, 0))
    grid_spec = pltpu.PrefetchScalarGridSpec(
        num_scalar_prefetch=1,
        grid=(n_tiles,),
        in_specs=[cur, cur, nxt, nxt,
                  pl.BlockSpec(memory_space=pl.ANY),
                  pl.BlockSpec((tm, d), lambda i, ps: (i, 0)),
                  pl.BlockSpec((tm, TOP_K), lambda i, ps: (i, 0)),
                  full((d, de)), full((d, de)), full((de, d)), full((1, d)), full((1, d))],
        out_specs=pl.BlockSpec((tm, d), lambda i, ps: (i, 0)),
        scratch_shapes=[pltpu.VMEM((2, TOP_K * tm * TILE_ROWS, LANES), F32), pltpu.SemaphoreType.DMA((2,))],
    )
    return pl.pallas_call(
        functools.partial(_combine_body, n_tiles, tm),
        out_shape=jax.ShapeDtypeStruct((t, d), F32),
        grid_spec=grid_spec,
        compiler_params=_params("arbitrary"),
        name="moe_combine_ln",
    )(pstart, e3, r3, e3, r3, ys, h, gate_t.T, sh_gate.astype(BF16), sh_up.astype(BF16), sh_down.astype(BF16),
      ln_g.reshape(1, d), ln_b.reshape(1, d))


def _moe_ln(h, h8, router_w, router_b, exp_gate, exp_up, exp_down, sh_gate, sh_up, sh_down, ln_g, ln_b):
    eidx_t, gate_t, rank_t, counts = _router(h, router_w, router_b)
    pstart, block_e, live, n_blocks = _block_plan(counts, eidx_t.size)
    xs = _dispatch(h8, pstart, live, eidx_t, rank_t, n_blocks)
    ys = _expert_blocks(xs, block_e, live, n_blocks, exp_gate, exp_up, exp_down)
    return _combine_ln(h, ys, pstart, eidx_t, rank_t, gate_t, sh_gate, sh_up, sh_down, ln_g, ln_b)


def _gdn_pre_body(ts, qkv_ref, w_ref, ba_ref, bat_ref, alog_r, dtb_r, alog_c, dtb_c,
                  q_out, k_out, v_out, beta_out, gcc_out, gcr_out, carry):
    i = pl.program_id(1)

    @pl.when(i == 0)
    def _():
        carry[...] = jnp.zeros_like(carry)

    sub8 = lax.broadcasted_iota(I32, (8, 1), 0)
    n_qk = C_KHEADS
    for hb in range(2 * C_KHEADS + C_VHEADS):
        sl = slice(hb * LANES, (hb + 1) * LANES)
        cur = qkv_ref[:, sl]
        tail = carry[:, sl]
        w = w_ref[:, sl]
        acc = cur * w[C_CONV - 1:C_CONV, :]
        for j in range(1, C_CONV):
            rolled = pltpu.roll(cur, j, 0)
            head = jnp.where(sub8 < j, pltpu.roll(tail, j, 0), rolled[0:8])
            shifted = jnp.concatenate([head, rolled[8:]], axis=0)
            acc = acc + shifted * w[C_CONV - 1 - j:C_CONV - j, :]
        carry[:, sl] = cur[ts - 8:ts, :]
        y = _silu(acc)
        if hb < 2 * n_qk:
            y = y * lax.rsqrt(jnp.sum(y * y, axis=1, keepdims=True) + 1e-6)
            if hb < n_qk:
                q_out[hb] = y * (C_DK ** -0.5)
            else:
                k_out[hb - n_qk] = y
        else:
            v_out[hb - 2 * n_qk] = y

    ba = ba_ref[...]
    beta_out[...] = _sigmoid(ba[:, 0:LANES])
    g_col = -jnp.exp(alog_r[...]) * _softplus(ba[:, LANES:2 * LANES] + dtb_r[...])
    rr = lax.broadcasted_iota(I32, (ts, ts), 0)
    cc = lax.broadcasted_iota(I32, (ts, ts), 1)
    same = (rr // C_CHUNK) == (cc // C_CHUNK)
    gcc_out[...] = _dot((same & (rr >= cc)).astype(F32), g_col, HIGHEST)
    g_row = -jnp.exp(alog_c[...]) * _softplus(bat_ref[C_VHEADS:2 * C_VHEADS, :] + dtb_c[...])
    gcr_out[...] = _dot(g_row, (same & (rr <= cc)).astype(F32), HIGHEST)


def _gdn_pre(p_main, ba, bat, conv_w, a_log, dt_bias, bn, s, *, ts=256):
    t = p_main.shape[0]
    nt = s // ts
    qkv_w = conv_w.shape[1]
    pad_r = lambda z: jnp.pad(z.astype(F32), (0, LANES - C_VHEADS)).reshape(1, LANES)
    col = lambda z: z.astype(F32).reshape(C_VHEADS, 1)
    full = lambda shape: pl.BlockSpec(shape, lambda b, i: (0,) * len(shape))
    hm = lambda nh: pl.BlockSpec((nh, ts, LANES), lambda b, i: (0, b * nt + i, 0))
    return pl.pallas_call(
        functools.partial(_gdn_pre_body, ts),
        out_shape=[jax.ShapeDtypeStruct((C_KHEADS, t, LANES), F32),
                   jax.ShapeDtypeStruct((C_KHEADS, t, LANES), F32),
                   jax.ShapeDtypeStruct((C_VHEADS, t, LANES), F32),
                   jax.ShapeDtypeStruct((t, LANES), F32),
                   jax.ShapeDtypeStruct((t, LANES), F32),
                   jax.ShapeDtypeStruct((C_VHEADS, t), F32)],
        grid=(bn, nt),
        in_specs=[pl.BlockSpec((ts, qkv_w), lambda b, i: (b * nt + i, 0)),
                  full((C_CONV, qkv_w)),
                  pl.BlockSpec((ts, 2 * LANES), lambda b, i: (b * nt + i, 0)),
                  pl.BlockSpec((2 * C_VHEADS, ts), lambda b, i: (0, b * nt + i)),
                  full((1, LANES)), full((1, LANES)), full((C_VHEADS, 1)), full((C_VHEADS, 1))],
        out_specs=[hm(C_KHEADS), hm(C_KHEADS), hm(C_VHEADS),
                   pl.BlockSpec((ts, LANES), lambda b, i: (b * nt + i, 0)),
                   pl.BlockSpec((ts, LANES), lambda b, i: (b * nt + i, 0)),
                   pl.BlockSpec((C_VHEADS, ts), lambda b, i: (0, b * nt + i))],
        scratch_shapes=[pltpu.VMEM((8, qkv_w), F32)],
        compiler_params=_params("parallel", "arbitrary"),
        name="gdn_pre",
    )(p_main, conv_w, ba, bat, pad_r(a_log), pad_r(dt_bias), col(a_log), col(dt_bias))


def _gdn_chunk_body(tb, gq, q_ref, k_ref, v_ref, z_ref, beta_ref, gcc_ref, gcr_ref, ng_ref, o_ref, state):
    c = C_CHUNK
    n = 2 * c
    hk0 = pl.program_id(1) * gq

    @pl.when(pl.program_id(2) == 0)
    def _():
        state[...] = jnp.zeros_like(state)

    row = lax.broadcasted_iota(I32, (n, n), 0)
    col = lax.broadcasted_iota(I32, (n, n), 1)
    same = (row // c) == (col // c)
    strict = same & (row > col)
    incl = same & (row >= col)
    lane = lax.broadcasted_iota(I32, (1, LANES), 1)
    top = lax.broadcasted_iota(I32, (n, 1), 0) < c
    ng = ng_ref[...]

    def pick(x, hv):
        return jnp.sum(jnp.where(lane == hv, x, 0.0), axis=1, keepdims=True)

    def one_head(ci, rs, hh):
        hk = hk0 + hh
        q, k = q_ref[hh, rs, :], k_ref[hh, rs, :]
        v = jnp.concatenate([v_ref[2 * hh, rs, :], v_ref[2 * hh + 1, rs, :]], axis=0)
        bt, gc = beta_ref[rs, :], gcc_ref[rs, :]
        beta = jnp.concatenate([pick(bt, 2 * hk), pick(bt, 2 * hk + 1)], axis=0)
        gcs = jnp.concatenate([pick(gc, 2 * hk), pick(gc, 2 * hk + 1)], axis=0)
        gcr = gcr_ref[hh, pl.ds(ci, 1), :]
        k2 = jnp.concatenate([k, k], axis=0)
        q2 = jnp.concatenate([q, q], axis=0)
        dec = jnp.exp(jnp.where(incl, gcs - gcr, -jnp.inf))
        kk = _bdot_nt(k2, k2)
        qk = _bdot_nt(q2, k2)
        a_blk = jnp.where(strict, kk * beta * dec, 0.0)
        attn = jnp.where(incl, qk * dec, 0.0)
        tinv = _tri_inv(a_blk, c)
        eg = jnp.exp(gcs)
        rhs = jnp.concatenate([v * beta, k2 * (beta * eg)], axis=1)
        sol = _bdot(tinv, rhs)
        u, w = sol[:, 0:LANES], sol[:, LANES:2 * LANES]
        qd = q2 * eg
        gl0, gl1 = gcs[c - 1:c, :], gcs[n - 1:n, :]
        kd = k2 * jnp.exp(jnp.where(top, gl0, gl1) - gcs)
        s0, s1 = state[2 * hh], state[2 * hh + 1]
        ws0 = _bdot(jnp.concatenate([w[0:c], qd[0:c]], axis=0), s0)
        ws1 = _bdot(jnp.concatenate([w[c:n], qd[c:n]], axis=0), s1)
        v_new = u - jnp.concatenate([ws0[0:c], ws1[0:c]], axis=0)
        o = jnp.concatenate([ws0[c:n], ws1[c:n]], axis=0) + _bdot(attn, v_new)
        state[2 * hh] = s0 * jnp.exp(gl0) + _bdot_tn(kd[0:c], v_new[0:c])
        state[2 * hh + 1] = s1 * jnp.exp(gl1) + _bdot_tn(kd[c:n], v_new[c:n])
        on = o * lax.rsqrt(jnp.mean(o * o, axis=1, keepdims=True) + C_NORM_EPS) * ng
        c0 = 2 * hh * LANES
        o_ref[rs, c0:c0 + LANES] = on[0:c] * _silu(z_ref[rs, c0:c0 + LANES])
        o_ref[rs, c0 + LANES:c0 + 2 * LANES] = on[c:n] * _silu(z_ref[rs, c0 + LANES:c0 + 2 * LANES])

    def chunk(ci, _):
        rs = pl.ds(pl.multiple_of(ci * c, c), c)
        for hh in range(gq):
            one_head(ci, rs, hh)
        return 0

    lax.fori_loop(0, tb // c, chunk, 0)


def _gdn_chunk(qh, kh, vh, p_main, beta_c, gc_c, gc_r, norm_g, bn, s, *, tb=512, gq=C_KHEADS):
    t = qh.shape[1]
    nt = s // tb
    z_off = (2 * C_KHEADS + C_VHEADS) // 2
    gcr = gc_r.reshape(C_KHEADS, 2, t // C_CHUNK, C_CHUNK).transpose(0, 2, 1, 3).reshape(
        C_KHEADS, t // C_CHUNK, 2 * C_CHUNK)
    tile = lambda f: pl.BlockSpec((gq, tb, LANES), f)
    return pl.pallas_call(
        functools.partial(_gdn_chunk_body, tb, gq),
        out_shape=jax.ShapeDtypeStruct((t, C_VHEADS * LANES), F32),
        grid=(bn, C_KHEADS // gq, nt),
        in_specs=[tile(lambda b, h, i: (h, b * nt + i, 0)),
                  tile(lambda b, h, i: (h, b * nt + i, 0)),
                  pl.BlockSpec((2 * gq, tb, LANES), lambda b, h, i: (h, b * nt + i, 0)),
                  pl.BlockSpec((tb, 2 * gq * LANES), lambda b, h, i: (b * nt + i, z_off // gq + h)),
                  pl.BlockSpec((tb, LANES), lambda b, h, i: (b * nt + i, 0)),
                  pl.BlockSpec((tb, LANES), lambda b, h, i: (b * nt + i, 0)),
                  pl.BlockSpec((gq, tb // C_CHUNK, 2 * C_CHUNK), lambda b, h, i: (h, b * nt + i, 0)),
                  pl.BlockSpec((1, LANES), lambda b, h, i: (0, 0))],
        out_specs=pl.BlockSpec((tb, 2 * gq * LANES), lambda b, h, i: (b * nt + i, h)),
        scratch_shapes=[pltpu.VMEM((2 * gq, C_DK, LANES), F32)],
        compiler_params=_params("parallel", "parallel", "arbitrary"),
        name="gdn_chunk",
    )(qh, kh, vh, p_main, beta_c, gc_c, gcr, norm_g.reshape(1, LANES))


def _attn_rwkv_layer(x2, bn, s, w_in, a_lambda, a_norm_g, b_mix, b_w0, b_w2, b_a0, b_a2, b_g2, b_k_k, b_k_a, b_r_k,
                     b_lnx_g, b_lnx_b, w_out, ln_g, ln_b, layer_idx):
    a_cols = 3 * A_HEADS * A_VDIM
    pa = _matmul(x2, w_in[:, :a_cols], tn=a_cols)
    pb = _matmul(x2, w_in[:, a_cols:], tn=(w_in.shape[1] - a_cols) // 2)
    ya = _diff_attention(pa, a_lambda, a_norm_g, bn, s, layer_idx)
    pre = _rwkv_pre(pb, b_mix, b_w0, b_w2, b_a0, b_a2, b_g2, b_k_k, b_k_a, b_r_k, bn, s)
    yb = _rwkv_scan(*pre, b_lnx_g, b_lnx_b, bn, s)
    wa = A_HEADS * A_VDIM
    return _matmul_res_ln([ya, yb], [w_out[:wa], w_out[wa:]], x2, ln_g, ln_b)


def _gdn_layer(x2, bn, s, w_in, conv_w, a_log, dt_bias, norm_g, w_out, ln_g, ln_b):
    qkv_w = conv_w.shape[1]
    main_w = qkv_w + C_VHEADS * LANES
    w_b = w_in[:, main_w:main_w + C_VHEADS]
    w_a = w_in[:, main_w + C_VHEADS:main_w + 2 * C_VHEADS]
    pad = lambda w: jnp.pad(w, ((0, 0), (0, LANES - C_VHEADS)))
    p_main = _matmul(x2, w_in[:, :main_w], tn=main_w // 4)
    ba = _matmul(x2, jnp.concatenate([pad(w_b), pad(w_a)], axis=1), tn=2 * LANES)
    bat = _matmul_t(jnp.concatenate([w_b, w_a], axis=1).T, x2)
    qh, kh, vh, beta_c, gc_c, gc_r = _gdn_pre(p_main, ba, bat, conv_w, a_log, dt_bias, bn, s)
    o = _gdn_chunk(qh, kh, vh, p_main, beta_c, gc_c, gc_r, norm_g, bn, s)
    return _matmul_res_ln([o], [w_out], x2, ln_g, ln_b)


def kernel(x, l0_w_in, l0_a_lambda, l0_a_norm_g, l0_b_mix, l0_b_w0, l0_b_w2, l0_b_a0, l0_b_a2, l0_b_g2, l0_b_k_k, l0_b_k_a, l0_b_r_k, l0_b_lnx_g, l0_b_lnx_b, l0_w_out, l0_ln1_g, l0_ln1_b, l0_router_w, l0_router_b, l0_exp_gate, l0_exp_up, l0_exp_down, l0_sh_gate, l0_sh_up, l0_sh_down, l0_ln2_g, l0_ln2_b, l1_w_in, l1_conv_w, l1_a_log, l1_dt_bias, l1_norm_g, l1_w_out, l1_ln1_g, l1_ln1_b, l1_router_w, l1_router_b, l1_exp_gate, l1_exp_up, l1_exp_down, l1_sh_gate, l1_sh_up, l1_sh_down, l1_ln2_g, l1_ln2_b):
    bn, s, d = x.shape
    x2 = x.reshape(bn * s, d)
    x2, x8 = _attn_rwkv_layer(x2, bn, s, l0_w_in, l0_a_lambda, l0_a_norm_g, l0_b_mix, l0_b_w0, l0_b_w2, l0_b_a0,
                              l0_b_a2, l0_b_g2, l0_b_k_k, l0_b_k_a, l0_b_r_k, l0_b_lnx_g, l0_b_lnx_b, l0_w_out,
                              l0_ln1_g, l0_ln1_b, 0)
    x2 = _moe_ln(x2, x8, l0_router_w, l0_router_b, l0_exp_gate, l0_exp_up, l0_exp_down, l0_sh_gate, l0_sh_up,
                 l0_sh_down, l0_ln2_g, l0_ln2_b)
    x2, x8 = _gdn_layer(x2, bn, s, l1_w_in, l1_conv_w, l1_a_log, l1_dt_bias, l1_norm_g, l1_w_out, l1_ln1_g,
                        l1_ln1_b)
    x2 = _moe_ln(x2, x8, l1_router_w, l1_router_b, l1_exp_gate, l1_exp_up, l1_exp_down, l1_sh_gate, l1_sh_up,
                 l1_sh_down, l1_ln2_g, l1_ln2_b)
    return x2.reshape(bn, s, d)
```
